```python
import jax, jax.numpy as jnp
from jax import lax
import numpy as np

D_MODEL = 2048
BATCH = 8
SEQ = 4096
DEPTH = 4

CHUNK = 64
HEAD_DIM = 128
N_HEADS_A = 6
N_HEADS_B = 6
N_HEADS_M = 4
BAND_CHUNKS = 8
REL_CLIP = 128
Q_BLOCK = 128
N_MEM = 256
N_EXPERTS = 16
N_GROUPS = 4
E_PER_GROUP = N_EXPERTS // N_GROUPS
TOP_K = 2
D_FF_EXPERT = 1024
MOE_BLOCK = 256
N_BRANCH = 3
EPS = 1e-6

W_A = N_HEADS_A * HEAD_DIM
W_B = N_HEADS_B * HEAD_DIM
W_M = N_HEADS_M * HEAD_DIM
OFF_QA = 0
OFF_KA = OFF_QA + W_A
OFF_VA = OFF_KA + W_A
OFF_QB = OFF_VA + W_A
OFF_KB = OFF_QB + W_B
OFF_VB = OFF_KB + W_B
OFF_FB = OFF_VB + W_B
OFF_QM = OFF_FB + N_HEADS_B
OFF_G = OFF_QM + W_M
N_IN = OFF_G + N_BRANCH * D_MODEL

kernel_name = "hybrid_chunk_forget_mem_moe_trunk"


def rms_norm(x, g):
    xf = x.astype(jnp.float32)
    y = xf * lax.rsqrt(jnp.mean(xf * xf, axis=-1, keepdims=True) + EPS)
    return (y * g.astype(jnp.float32)).astype(x.dtype)


def chunk_rel_attention(q, k, v, rel_bias):
    b, s, h, dh = q.shape
    n_chunks = s // CHUNK
    band = (BAND_CHUNKS + 1) * CHUNK
    pad = ((0, 0), (BAND_CHUNKS * CHUNK, 0), (0, 0), (0, 0))
    kp = jnp.pad(k, pad)
    vp = jnp.pad(v, pad)
    q_loc = jnp.arange(CHUNK)
    k_loc = jnp.arange(band) - BAND_CHUNKS * CHUNK
    rel = jnp.clip(k_loc[None, :] - q_loc[:, None], -REL_CLIP, REL_CLIP) + REL_CLIP
    bias = rel_bias.astype(jnp.float32)[:, rel]
    scale = HEAD_DIM ** -0.5

    def one_chunk(n):
        start = n * CHUNK
        qc = lax.dynamic_slice_in_dim(q, start, CHUNK, axis=1)
        kc = lax.dynamic_slice_in_dim(kp, start, band, axis=1)
        vc = lax.dynamic_slice_in_dim(vp, start, band, axis=1)
        sc = jnp.einsum('bqhd,bkhd->bhqk', qc, kc).astype(jnp.float32) * scale + bias[None]
        valid = (start + k_loc) >= 0
        sc = jnp.where(valid[None, None, None, :], sc, -jnp.inf)
        p = jax.nn.softmax(sc, axis=-1).astype(vc.dtype)
        return jnp.einsum('bhqk,bkhd->bqhd', p, vc)

    o = lax.map(one_chunk, jnp.arange(n_chunks))
    return o.transpose(1, 0, 2, 3, 4).reshape(b, s, h * dh)


def forgetting_attention(q, k, v, log_f):
    b, s, h, dh = q.shape
    f_cum = jnp.cumsum(log_f, axis=1).transpose(0, 2, 1)
    k_pos = jnp.arange(s)
    scale = HEAD_DIM ** -0.5

    def one_block(i):
        start = i * Q_BLOCK
        qb = lax.dynamic_slice_in_dim(q, start, Q_BLOCK, axis=1)
        fq = lax.dynamic_slice_in_dim(f_cum, start, Q_BLOCK, axis=2)
        sc = (jnp.einsum('bqhd,bkhd->bhqk', qb, k).astype(jnp.float32) * scale
              + fq[..., :, None] - f_cum[:, :, None, :])
        q_pos = start + jnp.arange(Q_BLOCK)
        sc = jnp.where((k_pos[None, :] <= q_pos[:, None])[None, None], sc, -jnp.inf)
        p = jax.nn.softmax(sc, axis=-1).astype(v.dtype)
        return jnp.einsum('bhqk,bkhd->bqhd', p, v)

    o = lax.map(one_block, jnp.arange(s // Q_BLOCK))
    return o.transpose(1, 0, 2, 3, 4).reshape(b, s, h * dh)


def memory_attention(q, k, v):
    b, s, h, dh = q.shape
    sc = jnp.einsum('bqhd,bkhd->bhqk', q, k).astype(jnp.float32) * (HEAD_DIM ** -0.5)
    p = jax.nn.softmax(sc, axis=-1).astype(v.dtype)
    return jnp.einsum('bhqk,bkhd->bqhd', p, v).reshape(b, s, h * dh)


def group_top2_route(xt, w_router, b_router):
    t = xt.shape[0]
    logits = jnp.matmul(xt, w_router).astype(jnp.float32) + b_router.astype(jnp.float32)
    scores = jax.nn.softmax(logits, axis=-1)
    grouped = scores.reshape(t, N_GROUPS, E_PER_GROUP)
    group_score = lax.top_k(grouped, TOP_K)[0].sum(-1)
    g_idx = jnp.argmax(group_score, axis=-1).astype(jnp.int32)
    in_group = jnp.take_along_axis(grouped, g_idx[:, None, None], axis=1)[:, 0]
    top_w, top_local = lax.top_k(in_group, TOP_K)
    expert_idx = g_idx[:, None] * E_PER_GROUP + top_local.astype(jnp.int32)
    top_w = top_w / jnp.sum(top_w, axis=-1, keepdims=True)
    return expert_idx, top_w


def moe_ffn(xt, expert_idx, top_w, w1, w3, w2):
    t, d = xt.shape
    n_assign = t * TOP_K
    flat_e = expert_idx.reshape(n_assign)
    flat_tok = jnp.repeat(jnp.arange(t, dtype=jnp.int32), TOP_K)
    flat_w = top_w.reshape(n_assign)
    order = jnp.argsort(flat_e)
    e_sorted = flat_e[order]
    tok_sorted = flat_tok[order]
    w_sorted = flat_w[order]
    counts = jnp.bincount(flat_e, length=N_EXPERTS)
    padded = (counts + MOE_BLOCK - 1) // MOE_BLOCK * MOE_BLOCK
    pad_end = jnp.cumsum(padded)
    pad_start = pad_end - padded
    start = jnp.cumsum(counts) - counts
    dest = pad_start[e_sorted] + jnp.arange(n_assign) - start[e_sorted]
    n_blocks = -(-n_assign // MOE_BLOCK) + N_EXPERTS
    buf_tok = jnp.zeros((n_blocks * MOE_BLOCK,), jnp.int32).at[dest].set(tok_sorted)
    block_e = jnp.minimum(
        jnp.searchsorted(pad_end, jnp.arange(n_blocks) * MOE_BLOCK, side='right'), N_EXPERTS - 1)
    xb = xt[buf_tok].reshape(n_blocks, MOE_BLOCK, d)

    def expert_block(args):
        xblk, e = args
        hid = jax.nn.silu(jnp.matmul(xblk, w1[e])) * jnp.matmul(xblk, w3[e])
        return jnp.matmul(hid, w2[e])

    yb = lax.map(expert_block, (xb, block_e)).reshape(n_blocks * MOE_BLOCK, d)
    y = yb[dest] * w_sorted[:, None].astype(yb.dtype)
    return jax.ops.segment_sum(y, tok_sorted, num_segments=t)


def setup_inputs(seed: int = 0) -> dict:
    key = jax.random.key(seed)
    ks = jax.random.split(key, 20)
    f32 = jnp.float32
    res_scale = (2.0 * DEPTH) ** -0.5
    n = jax.random.normal
    return {
        "x": n(ks[0], (BATCH, SEQ, D_MODEL), f32),
        "mem": n(ks[1], (BATCH, N_MEM, D_MODEL), f32),
        "norm_mix": 1.0 + 0.02 * n(ks[2], (DEPTH, D_MODEL), f32),
        "w_in": n(ks[3], (DEPTH, D_MODEL, N_IN), f32) * D_MODEL ** -0.5,
        "b_forget": jax.random.uniform(ks[4], (DEPTH, N_HEADS_B), f32, 1.0, 4.0),
        "rel_bias": 0.2 * n(ks[5], (DEPTH, N_HEADS_A, 2 * REL_CLIP + 1), f32),
        "norm_mem": 1.0 + 0.02 * n(ks[6], (DEPTH, D_MODEL), f32),
        "w_mem_kv": n(ks[7], (DEPTH, D_MODEL, 2 * W_M), f32) * D_MODEL ** -0.5,
        "w_br_a": n(ks[8], (DEPTH, W_A, D_MODEL), f32) * W_A ** -0.5,
        "w_br_b": n(ks[9], (DEPTH, W_B, D_MODEL), f32) * W_B ** -0.5,
        "w_br_m": n(ks[10], (DEPTH, W_M, D_MODEL), f32) * W_M ** -0.5,
        "w_out": n(ks[11], (DEPTH, D_MODEL, D_MODEL), f32) * D_MODEL ** -0.5 * res_scale,
        "norm_ffn": 1.0 + 0.02 * n(ks[12], (DEPTH, D_MODEL), f32),
        "w_router": n(ks[13], (D_MODEL, N_EXPERTS), f32) * D_MODEL ** -0.5,
        "b_router": 0.01 * n(ks[14], (N_EXPERTS,), f32),
        "w1": n(ks[15], (DEPTH, N_EXPERTS, D_MODEL, D_FF_EXPERT), f32) * D_MODEL ** -0.5,
        "w3": n(ks[16], (DEPTH, N_EXPERTS, D_MODEL, D_FF_EXPERT), f32) * D_MODEL ** -0.5,
        "w2": n(ks[17], (DEPTH, N_EXPERTS, D_FF_EXPERT, D_MODEL), f32) * D_FF_EXPERT ** -0.5 * res_scale,
        "norm_final": 1.0 + 0.02 * n(ks[18], (D_MODEL,), f32),
    }


def reference(x, mem, norm_mix, w_in, b_forget, rel_bias, norm_mem, w_mem_kv,
              w_br_a, w_br_b, w_br_m, w_out, norm_ffn, w_router, b_router,
              w1, w3, w2, norm_final):
    b, s, d = x.shape
    h = x
    for l in range(DEPTH):
        hn = rms_norm(h, norm_mix[l])
        proj = jnp.matmul(hn, w_in[l])
        q_a = proj[..., OFF_QA:OFF_KA].reshape(b, s, N_HEADS_A, HEAD_DIM)
        k_a = proj[..., OFF_KA:OFF_VA].reshape(b, s, N_HEADS_A, HEAD_DIM)
        v_a = proj[..., OFF_VA:OFF_QB].reshape(b, s, N_HEADS_A, HEAD_DIM)
        q_b = proj[..., OFF_QB:OFF_KB].reshape(b, s, N_HEADS_B, HEAD_DIM)
        k_b = proj[..., OFF_KB:OFF_VB].reshape(b, s, N_HEADS_B, HEAD_DIM)
        v_b = proj[..., OFF_VB:OFF_FB].reshape(b, s, N_HEADS_B, HEAD_DIM)
        f_logit = proj[..., OFF_FB:OFF_QM].astype(jnp.float32)
        q_m = proj[..., OFF_QM:OFF_G].reshape(b, s, N_HEADS_M, HEAD_DIM)
        gates = jax.nn.sigmoid(proj[..., OFF_G:].astype(jnp.float32)).astype(h.dtype)
        gates = gates.reshape(b, s, N_BRANCH, d)

        o_a = jnp.matmul(chunk_rel_attention(q_a, k_a, v_a, rel_bias[l]), w_br_a[l])

        log_f = jax.nn.log_sigmoid(f_logit + b_forget[l].astype(jnp.float32))
        o_b = jnp.matmul(forgetting_attention(q_b, k_b, v_b, log_f), w_br_b[l])

        kv_m = jnp.matmul(rms_norm(mem, norm_mem[l]), w_mem_kv[l])
        k_m = kv_m[..., :W_M].reshape(b, N_MEM, N_HEADS_M, HEAD_DIM)
        v_m = kv_m[..., W_M:].reshape(b, N_MEM, N_HEADS_M, HEAD_DIM)
        o_m = jnp.matmul(memory_attention(q_m, k_m, v_m), w_br_m[l])

        merged = gates[:, :, 0] * o_a + gates[:, :, 1] * o_b + gates[:, :, 2] * o_m
        h = h + jnp.matmul(merged, w_out[l])

        xt = rms_norm(h, norm_ffn[l]).reshape(b * s, d)
        expert_idx, top_w = group_top2_route(xt, w_router, b_router)
        h = h + moe_ffn(xt, expert_idx, top_w, w1[l], w3[l], w2[l]).reshape(b, s, d)
    return rms_norm(h, norm_final)
```

```python
import functools
import math

import jax
import jax.numpy as jnp
from jax import lax
from jax.experimental import pallas as pl
from jax.experimental.pallas import tpu as pltpu

CHUNK = 64
HEAD_DIM = 128
N_HEADS_A = 6
N_HEADS_B = 6
N_HEADS_M = 4
BAND_CHUNKS = 8
REL_CLIP = 128
N_EXPERTS = 16
N_GROUPS = 4
E_PER_GROUP = 4
MOE_BLOCK = 256
EPS = 1e-6
SCALE = HEAD_DIM ** -0.5

LANE = 128
NEG = -1e30
QA_BLOCK = 2 * CHUNK
BAND_BLOCKS = BAND_CHUNKS * CHUNK // QA_BLOCK + 1
N_QKV_BLOCKS = 3 * N_HEADS_A + 3 * N_HEADS_B

f32 = jnp.float32
bf16 = jnp.bfloat16


def _cparams(sem, vmem_mb):
    return pltpu.CompilerParams(dimension_semantics=sem, vmem_limit_bytes=vmem_mb * 1024 * 1024)


def _const_spec(shape):
    nd = len(shape)
    return pl.BlockSpec(shape, lambda *_: (0,) * nd, pipeline_mode=pl.Buffered(1))


def _proj_body(x_ref, g_ref, w_ref, wf_ref, o_ref, f_ref, xn_ref, *, n_sig, ncb):
    j = pl.program_id(1)

    @pl.when(j == 0)
    def _():
        x = x_ref[...]
        ms = jnp.mean(x * x, axis=-1, keepdims=True)
        xn = ((x * lax.rsqrt(ms + EPS)) * g_ref[...]).astype(bf16)
        xn_ref[...] = xn
        f_ref[...] = jnp.dot(xn, wf_ref[...], preferred_element_type=f32)

    acc = jnp.dot(xn_ref[...], w_ref[...], preferred_element_type=f32)

    @pl.when(j < n_sig)
    def _():
        for c in range(ncb):
            a = acc[:, c * LANE:(c + 1) * LANE]
            o_ref[c] = (1.0 / (1.0 + jnp.exp(-a))).astype(bf16)

    @pl.when(j >= n_sig)
    def _():
        for c in range(ncb):
            o_ref[c] = acc[:, c * LANE:(c + 1) * LANE].astype(bf16)


def _norm_proj(x, g, w, wf, *, n_sig_cols, tm, tn):
    m, d = x.shape
    n = w.shape[1]
    ncb = tn // LANE
    return pl.pallas_call(
        functools.partial(_proj_body, n_sig=n_sig_cols // tn, ncb=ncb),
        grid=(m // tm, n // tn),
        in_specs=[
            pl.BlockSpec((tm, d), lambda i, j: (i, 0)),
            pl.BlockSpec((1, d), lambda i, j: (0, 0)),
            pl.BlockSpec((d, tn), lambda i, j: (0, j)),
            pl.BlockSpec((d, LANE), lambda i, j: (0, 0)),
        ],
        out_specs=[
            pl.BlockSpec((ncb, tm, LANE), lambda i, j: (j, i, 0)),
            pl.BlockSpec((tm, LANE), lambda i, j: (i, 0)),
        ],
        out_shape=[
            jax.ShapeDtypeStruct((n // LANE, m, LANE), bf16),
            jax.ShapeDtypeStruct((m, LANE), f32),
        ],
        scratch_shapes=[pltpu.VMEM((tm, d), bf16)],
        compiler_params=_cparams(("parallel", "arbitrary"), 56),
        name="norm_proj",
    )(x, g, w, wf)


def _fcum_body(fl_ref, b_ref, fcol_ref, frow_ref, carry_ref, *, tb):
    @pl.when(pl.program_id(1) == 0)
    def _():
        carry_ref[...] = jnp.zeros_like(carry_ref)

    x = fl_ref[...] + b_ref[...]
    lf = jnp.minimum(x, 0.0) - jnp.log1p(jnp.exp(-jnp.abs(x)))
    hi = lf.astype(bf16)
    r1 = lf - hi.astype(f32)
    mid = r1.astype(bf16)
    lo = (r1 - mid.astype(f32)).astype(bf16)
    row = lax.broadcasted_iota(jnp.int32, (tb, tb), 0)
    col = lax.broadcasted_iota(jnp.int32, (tb, tb), 1)
    tri = (col <= row).astype(bf16)
    c = (jnp.dot(tri, hi, preferred_element_type=f32)
         + jnp.dot(tri, mid, preferred_element_type=f32)
         + jnp.dot(tri, lo, preferred_element_type=f32))
    c = c + carry_ref[...]
    carry_ref[...] = c[tb - 1:tb, :]
    fcol_ref[...] = c
    frow_ref[...] = c.T[:8, :]


def _forget_cumsum(flog, bias, *, batch, seq, tb):
    nsb = seq // tb
    return pl.pallas_call(
        functools.partial(_fcum_body, tb=tb),
        grid=(batch, nsb),
        in_specs=[
            pl.BlockSpec((tb, LANE), lambda b, s: (b * nsb + s, 0)),
            pl.BlockSpec((1, LANE), lambda b, s: (0, 0)),
        ],
        out_specs=[
            pl.BlockSpec((tb, LANE), lambda b, s: (b * nsb + s, 0)),
            pl.BlockSpec((None, None, 8, tb), lambda b, s: (b, s, 0, 0)),
        ],
        out_shape=[
            jax.ShapeDtypeStruct((batch * seq, LANE), f32),
            jax.ShapeDtypeStruct((batch, nsb, 8, tb), f32),
        ],
        scratch_shapes=[pltpu.VMEM((1, LANE), f32)],
        compiler_params=_cparams(("parallel", "arbitrary"), 32),
        name="forget_cumsum",
    )(flog, bias)


def _attn_a_body(q_ref, k_ref, v_ref, tb_ref, o_ref):
    i = pl.program_id(1)
    first = jnp.maximum(i - (BAND_BLOCKS - 1), 0)
    cb0 = jnp.maximum((BAND_BLOCKS - 1) - i, 0)
    for h in range(N_HEADS_A):
        q = q_ref[h]
        s_blocks = []
        for c in range(BAND_BLOCKS):
            ks = pl.multiple_of((first + c) * QA_BLOCK, QA_BLOCK)
            kb = k_ref[h, pl.ds(ks, QA_BLOCK), :]
            s = lax.dot_general(q, kb, (((1,), (1,)), ((), ())), preferred_element_type=f32)
            s_blocks.append(s * SCALE + tb_ref[h, cb0 + c])
        m = s_blocks[0].max(axis=-1, keepdims=True)
        for s in s_blocks[1:]:
            m = jnp.maximum(m, s.max(axis=-1, keepdims=True))
        l = jnp.zeros_like(m)
        acc = jnp.zeros((QA_BLOCK, HEAD_DIM), f32)
        for c in range(BAND_BLOCKS):
            p = jnp.exp(s_blocks[c] - m)
            l = l + p.sum(axis=-1, keepdims=True)
            ks = pl.multiple_of((first + c) * QA_BLOCK, QA_BLOCK)
            vb = v_ref[h, pl.ds(ks, QA_BLOCK), :]
            acc = acc + jnp.dot(p.astype(bf16), vb, preferred_element_type=f32)
        o_ref[h] = (acc / l).astype(bf16)


def _attn_a(proj, table, *, batch, seq, ng):
    nq = seq // QA_BLOCK
    base = ng // N_HEADS_A
    t = batch * seq
    return pl.pallas_call(
        _attn_a_body,
        grid=(batch, nq),
        in_specs=[
            pl.BlockSpec((N_HEADS_A, QA_BLOCK, LANE), lambda b, i: (base, b * nq + i, 0)),
            pl.BlockSpec((N_HEADS_A, seq, LANE), lambda b, i: (base + 1, b, 0)),
            pl.BlockSpec((N_HEADS_A, seq, LANE), lambda b, i: (base + 2, b, 0)),
            _const_spec(table.shape),
        ],
        out_specs=pl.BlockSpec((N_HEADS_A, QA_BLOCK, LANE), lambda b, i: (0, b * nq + i, 0)),
        out_shape=jax.ShapeDtypeStruct((N_HEADS_A, t, LANE), bf16),
        compiler_params=_cparams(("parallel", "arbitrary"), 48),
        name="attn_chunk",
    )(proj, proj, proj, table)


def _attn_b_body(q_ref, k_ref, v_ref, fcol_ref, frow_ref, o_ref, *, tq):
    i = pl.program_id(1)
    row = lax.broadcasted_iota(jnp.int32, (tq, tq), 0)
    col = lax.broadcasted_iota(jnp.int32, (tq, tq), 1)
    causal = col <= row

    for h in range(N_HEADS_B):
        q = q_ref[h]
        fq = fcol_ref[:, h:h + 1]

        def step(kb, carry, masked):
            m, l, acc = carry
            ks = pl.multiple_of(kb * tq, tq)
            k = k_ref[h, pl.ds(ks, tq), :]
            v = v_ref[h, pl.ds(ks, tq), :]
            fk = frow_ref[kb][h:h + 1, :]
            s = lax.dot_general(q, k, (((1,), (1,)), ((), ())), preferred_element_type=f32)
            s = s * SCALE + fq - fk
            if masked:
                s = jnp.where(causal, s, NEG)
            m_new = jnp.maximum(m, s.max(axis=-1, keepdims=True))
            alpha = jnp.exp(m - m_new)
            p = jnp.exp(s - m_new)
            l = alpha * l + p.sum(axis=-1, keepdims=True)
            acc = alpha * acc + jnp.dot(p.astype(bf16), v, preferred_element_type=f32)
            return m_new, l, acc

        init = (jnp.full((tq, 1), NEG, f32), jnp.zeros((tq, 1), f32), jnp.zeros((tq, HEAD_DIM), f32))
        carry = lax.fori_loop(0, i, lambda kb, c: step(kb, c, False), init)
        m, l, acc = step(i, carry, True)
        o_ref[h] = (acc / l).astype(bf16)


def _attn_b(proj, fcol, frow, *, batch, seq, ng, tq):
    nq = seq // tq
    base = ng // N_HEADS_B + 3
    t = batch * seq
    return pl.pallas_call(
        functools.partial(_attn_b_body, tq=tq),
        grid=(batch, nq),
        in_specs=[
            pl.BlockSpec((N_HEADS_B, tq, LANE), lambda b, i: (base, b * nq + i, 0)),
            pl.BlockSpec((N_HEADS_B, seq, LANE), lambda b, i: (base + 1, b, 0)),
            pl.BlockSpec((N_HEADS_B, seq, LANE), lambda b, i: (base + 2, b, 0)),
            pl.BlockSpec((tq, LANE), lambda b, i: (b * nq + i, 0)),
            pl.BlockSpec((None, nq, 8, tq), lambda b, i: (b, 0, 0, 0)),
        ],
        out_specs=pl.BlockSpec((N_HEADS_B, tq, LANE), lambda b, i: (0, b * nq + i, 0)),
        out_shape=jax.ShapeDtypeStruct((N_HEADS_B, t, LANE), bf16),
        compiler_params=_cparams(("parallel", "arbitrary"), 56),
        name="attn_forget",
    )(proj, proj, proj, fcol, frow)


def _attn_m_body(q_ref, k_ref, v_ref, o_ref):
    for h in range(N_HEADS_M):
        s = lax.dot_general(q_ref[h], k_ref[h], (((1,), (1,)), ((), ())), preferred_element_type=f32)
        s = s * SCALE
        m = s.max(axis=-1, keepdims=True)
        p = jnp.exp(s - m)
        l = p.sum(axis=-1, keepdims=True)
        acc = jnp.dot(p.astype(bf16), v_ref[h], preferred_element_type=f32)
        o_ref[h] = (acc / l).astype(bf16)


def _attn_m(proj, kv, *, batch, seq, n_mem, ng, tq):
    nq = seq // tq
    base = (ng + N_QKV_BLOCKS) // N_HEADS_M
    t = batch * seq
    return pl.pallas_call(
        _attn_m_body,
        grid=(batch, nq),
        in_specs=[
            pl.BlockSpec((N_HEADS_M, tq, LANE), lambda b, i: (base, b * nq + i, 0)),
            pl.BlockSpec((N_HEADS_M, n_mem, LANE), lambda b, i: (0, b, 0)),
            pl.BlockSpec((N_HEADS_M, n_mem, LANE), lambda b, i: (1, b, 0)),
        ],
        out_specs=pl.BlockSpec((N_HEADS_M, tq, LANE), lambda b, i: (0, b * nq + i, 0)),
        out_shape=jax.ShapeDtypeStruct((N_HEADS_M, t, LANE), bf16),
        compiler_params=_cparams(("parallel", "arbitrary"), 32),
        name="attn_mem",
    )(proj, kv, kv)


def _merge_body(h_ref, g_ref, oa_ref, ob_ref, om_ref, wa_ref, wb_ref, wm_ref, wo_ref,
                gn_ref, wr_ref, br_ref, hn_ref, xp_ref, lt_ref, *, d):
    ndb = d // LANE

    def heads(ref, n):
        return jnp.concatenate([ref[h] for h in range(n)], axis=-1)

    def gate(br):
        return jnp.concatenate([g_ref[br * ndb + c] for c in range(ndb)], axis=-1).astype(f32)

    o_a = jnp.dot(heads(oa_ref, N_HEADS_A), wa_ref[...], preferred_element_type=f32)
    merged = gate(0) * o_a
    o_b = jnp.dot(heads(ob_ref, N_HEADS_B), wb_ref[...], preferred_element_type=f32)
    merged = merged + gate(1) * o_b
    o_m = jnp.dot(heads(om_ref, N_HEADS_M), wm_ref[...], preferred_element_type=f32)
    merged = merged + gate(2) * o_m
    hn = h_ref[...] + jnp.dot(merged.astype(bf16), wo_ref[...], preferred_element_type=f32)
    hn_ref[...] = hn

    ms = jnp.mean(hn * hn, axis=-1, keepdims=True)
    xn = ((hn * lax.rsqrt(ms + EPS)) * gn_ref[...]).astype(bf16)
    logits = jnp.dot(xn, wr_ref[...], preferred_element_type=f32) + br_ref[...]
    lt_ref[...] = logits.T[:N_EXPERTS, :]
    xf = xn.astype(f32)
    lo = pltpu.bitcast(xf[:, :d // 2], jnp.uint32)
    hi = pltpu.bitcast(xf[:, d // 2:], jnp.uint32)
    xp_ref[...] = lax.shift_right_logical(lo, jnp.uint32(16)) | (hi & jnp.uint32(0xFFFF0000))


def _merge(h, proj, oa, ob, om, wa, wb, wm, wo, gn, wr, br, *, tm):
    t, d = h.shape
    ng = 3 * d // LANE
    return pl.pallas_call(
        functools.partial(_merge_body, d=d),
        grid=(t // tm,),
        in_specs=[
            pl.BlockSpec((tm, d), lambda i: (i, 0)),
            pl.BlockSpec((ng, tm, LANE), lambda i: (0, i, 0)),
            pl.BlockSpec((N_HEADS_A, tm, LANE), lambda i: (0, i, 0)),
            pl.BlockSpec((N_HEADS_B, tm, LANE), lambda i: (0, i, 0)),
            pl.BlockSpec((N_HEADS_M, tm, LANE), lambda i: (0, i, 0)),
            _const_spec(wa.shape), _const_spec(wb.shape), _const_spec(wm.shape), _const_spec(wo.shape),
            _const_spec(gn.shape), _const_spec(wr.shape), _const_spec(br.shape),
        ],
        out_specs=[
            pl.BlockSpec((tm, d), lambda i: (i, 0)),
            pl.BlockSpec((tm, d // 2), lambda i: (i, 0)),
            pl.BlockSpec((N_EXPERTS, tm), lambda i: (0, i)),
        ],
        out_shape=[
            jax.ShapeDtypeStruct((t, d), f32),
            jax.ShapeDtypeStruct((t, d // 2), jnp.uint32),
            jax.ShapeDtypeStruct((N_EXPERTS, t), f32),
        ],
        compiler_params=_cparams(("parallel",), 56),
        name="merge_out_router",
    )(h, proj, oa, ob, om, wa, wb, wm, wo, gn, wr, br)


def _route_body(lt_ref, idx_ref, wt_ref, be_ref, nu_ref, run_ref, tot_ref, *, tn, nbp):
    phase = pl.program_id(0)
    i = pl.program_id(1)
    last = pl.num_programs(1) - 1

    @pl.when(i == 0)
    def _():
        run_ref[...] = jnp.zeros_like(run_ref)

    x = lt_ref[...]
    ex = jnp.exp(x - x.max(axis=0, keepdims=True))
    sc = ex / ex.sum(axis=0, keepdims=True)

    def top2(rows):
        m1 = functools.reduce(jnp.maximum, rows)
        i1 = jnp.full_like(m1, float(len(rows) - 1))
        for j in range(len(rows) - 2, -1, -1):
            i1 = jnp.where(rows[j] == m1, float(j), i1)
        rest = [jnp.where(i1 == float(j), -1.0, r) for j, r in enumerate(rows)]
        m2 = functools.reduce(jnp.maximum, rest)
        i2 = jnp.full_like(m2, float(len(rows) - 1))
        for j in range(len(rows) - 2, -1, -1):
            i2 = jnp.where(rest[j] == m2, float(j), i2)
        return m1, i1, m2, i2

    groups = [top2([sc[g * E_PER_GROUP + j:g * E_PER_GROUP + j + 1, :] for j in range(E_PER_GROUP)])
              for g in range(N_GROUPS)]
    gs = [g[0] + g[2] for g in groups]
    best = functools.reduce(jnp.maximum, gs)
    sel = [groups[N_GROUPS - 1][k] for k in range(4)]
    gi = jnp.full_like(best, float(N_GROUPS - 1))
    for g in range(N_GROUPS - 2, -1, -1):
        hit = gs[g] == best
        sel = [jnp.where(hit, groups[g][k], sel[k]) for k in range(4)]
        gi = jnp.where(hit, float(g), gi)
    m1, i1, m2, i2 = sel
    e0 = gi * E_PER_GROUP + i1
    e1 = gi * E_PER_GROUP + i2
    wsum = m1 + m2
    w0 = m1 / wsum
    w1 = m2 / wsum

    erow = lax.broadcasted_iota(jnp.int32, (N_EXPERTS, 1), 0).astype(f32)
    oh0 = (erow == e0).astype(f32)
    oh1 = (erow == e1).astype(f32)
    sel_mask = oh0 + oh1
    r = lax.broadcasted_iota(jnp.int32, (tn, tn), 0)
    c = lax.broadcasted_iota(jnp.int32, (tn, tn), 1)
    before = (r < c).astype(bf16)
    cnt = jnp.dot(sel_mask.astype(bf16), before, preferred_element_type=f32) + run_ref[:, 0:1]
    run_ref[...] = run_ref[...] + sel_mask.sum(axis=1, keepdims=True)

    @pl.when(jnp.logical_and(phase == 0, i == last))
    def _():
        tot_ref[...] = run_ref[...]

    @pl.when(phase == 1)
    def _():
        tot = tot_ref[:, 0:1]
        padded = jnp.floor((tot + (MOE_BLOCK - 1)) / MOE_BLOCK) * MOE_BLOCK
        start = jnp.zeros_like(padded)
        for e in range(N_EXPERTS - 1):
            start = start + jnp.where(erow > float(e), padded[e:e + 1, :], 0.0)
        slot = start + cnt
        d0 = (oh0 * slot).sum(axis=0, keepdims=True)
        d1 = (oh1 * slot).sum(axis=0, keepdims=True)
        r8 = lax.broadcasted_iota(jnp.int32, (8, 1), 0)
        rows = jnp.where(r8 == 0, e0, jnp.where(r8 == 1, e1, jnp.where(r8 == 2, d0, jnp.where(r8 == 3, d1, 0.0))))
        idx_ref[...] = rows.astype(jnp.int32)
        rl = lax.broadcasted_iota(jnp.int32, (LANE, 1), 0)
        wt_ref[...] = jnp.where(rl == 0, w0, jnp.where(rl == 1, w1, 0.0)).T
        end = start + padded
        blk = lax.broadcasted_iota(jnp.int32, (1, nbp), 1).astype(f32) * MOE_BLOCK
        be = (end <= blk).astype(f32).sum(axis=0, keepdims=True)
        be_ref[...] = jnp.minimum(be, N_EXPERTS - 1.0).astype(jnp.int32)
        nu_ref[...] = jnp.broadcast_to(end[N_EXPERTS - 1:, :] / MOE_BLOCK, (1, LANE)).astype(jnp.int32)


def _route(lt, *, tn, nbp):
    t = lt.shape[1]
    return pl.pallas_call(
        functools.partial(_route_body, tn=tn, nbp=nbp),
        grid=(2, t // tn),
        in_specs=[pl.BlockSpec((N_EXPERTS, tn), lambda p, i: (0, i))],
        out_specs=[
            pl.BlockSpec((8, tn), lambda p, i: (0, i * p)),
            pl.BlockSpec((tn, LANE), lambda p, i: (i * p, 0)),
            pl.BlockSpec((1, nbp), lambda p, i: (0, 0)),
            pl.BlockSpec((1, LANE), lambda p, i: (0, 0)),
        ],
        out_shape=[
            jax.ShapeDtypeStruct((8, t), jnp.int32),
            jax.ShapeDtypeStruct((t, LANE), f32),
            jax.ShapeDtypeStruct((1, nbp), jnp.int32),
            jax.ShapeDtypeStruct((1, LANE), jnp.int32),
        ],
        scratch_shapes=[pltpu.VMEM((N_EXPERTS, LANE), f32), pltpu.VMEM((N_EXPERTS, LANE), f32)],
        compiler_params=_cparams(("arbitrary", "arbitrary"), 32),
        name="route",
    )(lt)


def _dispatch_body(dest_ref, x_ref, zeros_ref, xb_ref, sem, *, tc, t):
    del zeros_ref
    base = pl.program_id(0) * tc

    def copy(r, d):
        return pltpu.make_async_copy(x_ref.at[pl.ds(r, 1), :], xb_ref.at[pl.ds(d, 1), :], sem)

    def issue(r, carry):
        copy(r, dest_ref[base + r]).start()
        copy(r, dest_ref[t + base + r]).start()
        return carry

    lax.fori_loop(0, tc, issue, 0)

    def drain(r, carry):
        copy(0, 0).wait()
        copy(0, 0).wait()
        return carry

    lax.fori_loop(0, tc, drain, 0)


def _dispatch(dest, xp, zeros, *, tc):
    t, dh = xp.shape
    grid_spec = pltpu.PrefetchScalarGridSpec(
        num_scalar_prefetch=1,
        grid=(t // tc,),
        in_specs=[
            pl.BlockSpec((tc, dh), lambda i, dest: (i, 0)),
            pl.BlockSpec(memory_space=pl.ANY),
        ],
        out_specs=pl.BlockSpec(memory_space=pl.ANY),
        scratch_shapes=[pltpu.SemaphoreType.DMA],
    )
    return pl.pallas_call(
        functools.partial(_dispatch_body, tc=tc, t=t),
        grid_spec=grid_spec,
        out_shape=jax.ShapeDtypeStruct(zeros.shape, zeros.dtype),
        input_output_aliases={2: 0},
        compiler_params=_cparams(("arbitrary",), 32),
        name="dispatch",
    )(dest, xp, zeros)


def _ffn_body(be_ref, nu_ref, x_ref, w1_ref, w3_ref, w2_ref, y_ref, *, dh):
    del be_ref
    b = pl.program_id(0)

    @pl.when(b < nu_ref[0])
    def _():
        u = x_ref[...]
        lo = pltpu.bitcast(lax.shift_left(u, jnp.uint32(16)), f32).astype(bf16)
        hi = pltpu.bitcast(u & jnp.uint32(0xFFFF0000), f32).astype(bf16)
        h1 = (jnp.dot(lo, w1_ref[:dh, :], preferred_element_type=f32)
              + jnp.dot(hi, w1_ref[dh:, :], preferred_element_type=f32))
        h3 = (jnp.dot(lo, w3_ref[:dh, :], preferred_element_type=f32)
              + jnp.dot(hi, w3_ref[dh:, :], preferred_element_type=f32))
        hid = (h1 / (1.0 + jnp.exp(-h1))) * h3
        y_ref[...] = jnp.dot(hid.astype(bf16), w2_ref[...], preferred_element_type=f32)

    @pl.when(b >= nu_ref[0])
    def _():
        y_ref[...] = jnp.zeros_like(y_ref)


def _ffn(block_e, n_used, xb, w1, w3, w2):
    ns, dh = xb.shape
    _, d, f = w1.shape
    nb = ns // MOE_BLOCK
    grid_spec = pltpu.PrefetchScalarGridSpec(
        num_scalar_prefetch=2,
        grid=(nb,),
        in_specs=[
            pl.BlockSpec((MOE_BLOCK, dh), lambda b, be, nu: (b, 0)),
            pl.BlockSpec((None, d, f), lambda b, be, nu: (be[b], 0, 0)),
            pl.BlockSpec((None, d, f), lambda b, be, nu: (be[b], 0, 0)),
            pl.BlockSpec((None, f, d), lambda b, be, nu: (be[b], 0, 0)),
        ],
        out_specs=pl.BlockSpec((MOE_BLOCK, d), lambda b, be, nu: (b, 0)),
    )
    return pl.pallas_call(
        functools.partial(_ffn_body, dh=dh),
        grid_spec=grid_spec,
        out_shape=jax.ShapeDtypeStruct((ns, d), f32),
        compiler_params=_cparams(("arbitrary",), 56),
        name="expert_ffn",
    )(block_e, n_used, xb, w1, w3, w2)


def _combine_body(dest_ref, h_ref, wt_ref, y_ref, o_ref, buf0, buf1, sem, *, tc, t):
    base = pl.program_id(0) * tc

    def copy(d, buf, r):
        return pltpu.make_async_copy(y_ref.at[pl.ds(d, 1), :], buf.at[pl.ds(r, 1), :], sem)

    def issue(r, carry):
        copy(dest_ref[base + r], buf0, r).start()
        copy(dest_ref[t + base + r], buf1, r).start()
        return carry

    lax.fori_loop(0, tc, issue, 0)

    def drain(r, carry):
        copy(0, buf0, 0).wait()
        copy(0, buf1, 0).wait()
        return carry

    lax.fori_loop(0, tc, drain, 0)
    w = wt_ref[...]
    o_ref[...] = h_ref[...] + (w[:, 0:1] * buf0[...] + w[:, 1:2] * buf1[...])


def _combine(dest, h, wt, y, *, tc):
    t, d = h.shape
    grid_spec = pltpu.PrefetchScalarGridSpec(
        num_scalar_prefetch=1,
        grid=(t // tc,),
        in_specs=[
            pl.BlockSpec((tc, d), lambda i, dest: (i, 0)),
            pl.BlockSpec((tc, LANE), lambda i, dest: (i, 0)),
            pl.BlockSpec(memory_space=pl.ANY),
        ],
        out_specs=pl.BlockSpec((tc, d), lambda i, dest: (i, 0)),
        scratch_shapes=[pltpu.VMEM((tc, d), f32), pltpu.VMEM((tc, d), f32), pltpu.SemaphoreType.DMA],
    )
    return pl.pallas_call(
        functools.partial(_combine_body, tc=tc, t=t),
        grid_spec=grid_spec,
        out_shape=jax.ShapeDtypeStruct((t, d), f32),
        compiler_params=_cparams(("arbitrary",), 48),
        name="combine",
    )(dest, h, wt, y)


def _final_norm_body(x_ref, g_ref, o_ref):
    x = x_ref[...]
    ms = jnp.mean(x * x, axis=-1, keepdims=True)
    o_ref[...] = (x * lax.rsqrt(ms + EPS)) * g_ref[...]


def _final_norm(x, g, *, tm):
    t, d = x.shape
    return pl.pallas_call(
        _final_norm_body,
        grid=(t // tm,),
        in_specs=[pl.BlockSpec((tm, d), lambda i: (i, 0)), pl.BlockSpec((1, d), lambda i: (0, 0))],
        out_specs=pl.BlockSpec((tm, d), lambda i: (i, 0)),
        out_shape=jax.ShapeDtypeStruct((t, d), f32),
        compiler_params=_cparams(("parallel",), 32),
        name="final_norm",
    )(x, g)


def _rel_bias_table(rel_bias_l):
    ncol = (2 * BAND_BLOCKS - 1) * QA_BLOCK
    row = jnp.arange(QA_BLOCK)[:, None]
    u = jnp.arange(ncol)[None, :]
    rel = jnp.clip(u - (BAND_BLOCKS - 1) * QA_BLOCK - row, -REL_CLIP, REL_CLIP) + REL_CLIP
    kc = u // CHUNK
    qc = row // CHUNK + BAND_CHUNKS
    valid = (kc >= qc - BAND_CHUNKS) & (kc <= qc)
    tab = jnp.where(valid[None], rel_bias_l.astype(f32)[:, rel], NEG)
    return tab.reshape(N_HEADS_A, QA_BLOCK, 2 * BAND_BLOCKS - 1, QA_BLOCK).transpose(0, 2, 1, 3)


def _pad_cols(w, n):
    return jnp.pad(w, ((0, 0), (0, n - w.shape[1])))


def kernel(x, mem, norm_mix, w_in, b_forget, rel_bias, norm_mem, w_mem_kv, w_br_a, w_br_b, w_br_m,
           w_out, norm_ffn, w_router, b_router, w1, w3, w2, norm_final):
    batch, seq, d = x.shape
    n_mem = mem.shape[1]
    depth = w_in.shape[0]
    t = batch * seq
    wa_cols = N_HEADS_A * HEAD_DIM
    wb_cols = N_HEADS_B * HEAD_DIM
    wm_cols = N_HEADS_M * HEAD_DIM
    off_fb = 3 * wa_cols + 3 * wb_cols
    off_qm = off_fb + N_HEADS_B
    off_g = off_qm + wm_cols
    ng = 3 * d // LANE
    assert ng % N_HEADS_A == 0 and (ng + N_QKV_BLOCKS) % N_HEADS_M == 0
    assert seq % 512 == 0 and n_mem % LANE == 0

    tn_proj = min(1024, math.gcd(3 * d, off_fb + wm_cols))
    tm_proj = min(1024, t)
    tq_b = 512
    tc = min(512, t)
    n_slots = (-(-(t * 2) // MOE_BLOCK) + N_EXPERTS) * MOE_BLOCK
    nb = n_slots // MOE_BLOCK
    nbp = -(-nb // LANE) * LANE

    h = x.reshape(t, d)
    memf = mem.reshape(batch * n_mem, d)
    wr = _pad_cols(w_router, LANE).astype(bf16)
    br = _pad_cols(b_router.reshape(1, -1), LANE).astype(f32)
    zeros_f = jnp.zeros((d, LANE), bf16)

    for l in range(depth):
        wl = w_in[l]
        w_main = jnp.concatenate([wl[:, off_g:], wl[:, :off_fb], wl[:, off_qm:off_g]], axis=1).astype(bf16)
        w_f = _pad_cols(wl[:, off_fb:off_qm], LANE).astype(bf16)
        proj, flog = _norm_proj(h, norm_mix[l].reshape(1, d), w_main, w_f,
                                n_sig_cols=3 * d, tm=tm_proj, tn=tn_proj)

        bias_f = _pad_cols(b_forget[l].reshape(1, -1), LANE).astype(f32)
        fcol, frow = _forget_cumsum(flog, bias_f, batch=batch, seq=seq, tb=tq_b)

        kv, _ = _norm_proj(memf, norm_mem[l].reshape(1, d), w_mem_kv[l].astype(bf16), zeros_f,
                           n_sig_cols=0, tm=min(1024, batch * n_mem), tn=wm_cols)

        oa = _attn_a(proj, _rel_bias_table(rel_bias[l]), batch=batch, seq=seq, ng=ng)
        ob = _attn_b(proj, fcol, frow, batch=batch, seq=seq, ng=ng, tq=tq_b)
        om = _attn_m(proj, kv, batch=batch, seq=seq, n_mem=n_mem, ng=ng, tq=512)

        h, xp, lt = _merge(h, proj, oa, ob, om,
                           w_br_a[l].astype(bf16), w_br_b[l].astype(bf16), w_br_m[l].astype(bf16),
                           w_out[l].astype(bf16), norm_ffn[l].reshape(1, d), wr, br, tm=256)

        idx, wt, block_e, n_used = _route(lt, tn=512, nbp=nbp)
        dest = idx[2:4].reshape(2 * t)
        xb = _dispatch(dest, xp, jnp.zeros((n_slots, d // 2), jnp.uint32), tc=tc)
        yb = _ffn(block_e[0, :nb], n_used[0, :1], xb, w1[l].astype(bf16), w3[l].astype(bf16), w2[l].astype(bf16))
        h = _combine(dest, h, wt, yb, tc=tc)

    return _final_norm(h, norm_final.reshape(1, d), tm=min(512, t)).reshape(batch, seq, d)
```

```python
import functools
import math

import jax
import jax.numpy as jnp
from jax import lax
from jax.experimental import pallas as pl
from jax.experimental.pallas import tpu as pltpu

CHUNK = 64
HEAD_DIM = 128
N_HEADS_A = 6
N_HEADS_B = 6
N_HEADS_M = 4
BAND_CHUNKS = 8
REL_CLIP = 128
N_EXPERTS = 16
N_GROUPS = 4
E_PER_GROUP = 4
MOE_BLOCK = 256
EPS = 1e-6
SCALE = HEAD_DIM ** -0.5
LOG2E = 1.4426950408889634

LANE = 128
BF16_ROWS = 16
NEG = -1e30
QA_BLOCK = 2 * CHUNK
BAND_BLOCKS = BAND_CHUNKS * CHUNK // QA_BLOCK + 1
N_QKV_BLOCKS = 3 * N_HEADS_A + 3 * N_HEADS_B
W_A = N_HEADS_A * HEAD_DIM
W_B = N_HEADS_B * HEAD_DIM
W_M = N_HEADS_M * HEAD_DIM
OFF_FB = 3 * W_A + 3 * W_B
OFF_QM = OFF_FB + N_HEADS_B
OFF_G = OFF_QM + W_M

f32 = jnp.float32
bf16 = jnp.bfloat16


def _cparams(sem, vmem_mb):
    return pltpu.CompilerParams(dimension_semantics=sem, vmem_limit_bytes=vmem_mb * 1024 * 1024)


def _layer_spec(shape, layer):
    nd = len(shape) - 1
    return pl.BlockSpec((None,) + tuple(shape[1:]), lambda *_: (layer,) + (0,) * nd,
                        pipeline_mode=pl.Buffered(1))


def _const_spec(shape):
    nd = len(shape)
    return pl.BlockSpec(shape, lambda *_: (0,) * nd, pipeline_mode=pl.Buffered(1))


def _prep_body(w_ref, o_ref, f_ref, *, d):
    o_ref[:, :3 * d] = w_ref[:, OFF_G:OFF_G + 3 * d].astype(bf16)
    o_ref[:, 3 * d:3 * d + OFF_FB] = w_ref[:, :OFF_FB].astype(bf16)
    o_ref[:, 3 * d + OFF_FB:] = w_ref[:, OFF_QM:OFF_G].astype(bf16)
    lane = lax.broadcasted_iota(jnp.int32, (1, LANE), 1)
    f_ref[...] = jnp.where(lane < N_HEADS_B, w_ref[:, OFF_FB:OFF_FB + LANE], 0.0).astype(bf16)


def _prep_w_in(w_in, *, tk):
    depth, d, n_in = w_in.shape
    nm = 3 * d + OFF_FB + W_M
    return pl.pallas_call(
        functools.partial(_prep_body, d=d),
        grid=(depth, d // tk),
        in_specs=[pl.BlockSpec((None, tk, n_in), lambda l, k: (l, k, 0))],
        out_specs=[
            pl.BlockSpec((None, tk, nm), lambda l, k: (l, k, 0)),
            pl.BlockSpec((None, tk, LANE), lambda l, k: (l, k, 0)),
        ],
        out_shape=[
            jax.ShapeDtypeStruct((depth, d, nm), bf16),
            jax.ShapeDtypeStruct((depth, d, LANE), bf16),
        ],
        compiler_params=_cparams(("parallel", "parallel"), 48),
        name="prep_w_in",
    )(w_in)


def _proj_body(x_ref, g_ref, w_ref, wf_ref, o_ref, f_ref, xn_ref, *, n_sig, ncb):
    j = pl.program_id(1)

    @pl.when(j == 0)
    def _():
        x = x_ref[...]
        ms = jnp.mean(x * x, axis=-1, keepdims=True)
        xn = ((x * lax.rsqrt(ms + EPS)) * g_ref[...]).astype(bf16)
        xn_ref[...] = xn
        f_ref[...] = jnp.dot(xn, wf_ref[...], preferred_element_type=f32)

    acc = jnp.dot(xn_ref[...], w_ref[...], preferred_element_type=f32)

    @pl.when(j < n_sig)
    def _():
        for c in range(ncb):
            a = acc[:, c * LANE:(c + 1) * LANE]
            o_ref[c] = (1.0 / (1.0 + jnp.exp(-a))).astype(bf16)

    @pl.when(j >= n_sig)
    def _():
        for c in range(ncb):
            o_ref[c] = acc[:, c * LANE:(c + 1) * LANE].astype(bf16)


def _norm_proj(x, g, w, wf, layer, *, n_sig_cols, tm, tn):
    m, d = x.shape
    n = w.shape[2]
    ncb = tn // LANE
    return pl.pallas_call(
        functools.partial(_proj_body, n_sig=n_sig_cols // tn, ncb=ncb),
        grid=(m // tm, n // tn),
        in_specs=[
            pl.BlockSpec((tm, d), lambda i, j: (i, 0)),
            pl.BlockSpec((None, 1, d), lambda i, j: (layer, 0, 0)),
            pl.BlockSpec((None, d, tn), lambda i, j: (layer, 0, j)),
            pl.BlockSpec((None, d, LANE), lambda i, j: (layer, 0, 0)),
        ],
        out_specs=[
            pl.BlockSpec((ncb, tm, LANE), lambda i, j: (j, i, 0)),
            pl.BlockSpec((tm, LANE), lambda i, j: (i, 0)),
        ],
        out_shape=[
            jax.ShapeDtypeStruct((n // LANE, m, LANE), bf16),
            jax.ShapeDtypeStruct((m, LANE), f32),
        ],
        scratch_shapes=[pltpu.VMEM((tm, d), bf16)],
        compiler_params=_cparams(("parallel", "arbitrary"), 56),
        name="norm_proj",
    )(x, g, w, wf)


def _split3(x):
    hi = x.astype(bf16)
    r1 = x - hi.astype(f32)
    mid = r1.astype(bf16)
    lo = (r1 - mid.astype(f32)).astype(bf16)
    return hi, mid, lo


def _fcum_body(fl_ref, b_ref, qa_ref, ka_ref, carry_ref, *, tb):
    @pl.when(pl.program_id(1) == 0)
    def _():
        carry_ref[...] = jnp.zeros_like(carry_ref)

    x = fl_ref[...] + b_ref[...]
    lf = jnp.minimum(x, 0.0) - jnp.log1p(jnp.exp(-jnp.abs(x)))
    hi, mid, lo = _split3(lf)
    row = lax.broadcasted_iota(jnp.int32, (tb, tb), 0)
    col = lax.broadcasted_iota(jnp.int32, (tb, tb), 1)
    tri = (col <= row).astype(bf16)
    c = (jnp.dot(tri, hi, preferred_element_type=f32)
         + jnp.dot(tri, mid, preferred_element_type=f32)
         + jnp.dot(tri, lo, preferred_element_type=f32))
    c = c + carry_ref[...]
    carry_ref[...] = c[tb - 1:tb, :]

    lane = lax.broadcasted_iota(jnp.int32, (1, LANE), 1)
    for h in range(N_HEADS_B):
        fh, fm, fl = (p.astype(f32) for p in _split3(c[:, h:h + 1] * LOG2E))
        qa = jnp.where(lane == 0, fh, jnp.where(lane == 1, fm, jnp.where(lane == 2, fl,
                       jnp.where(lane < 6, 1.0, 0.0))))
        ka = jnp.where(lane < 3, 1.0, jnp.where(lane == 3, -fh, jnp.where(lane == 4, -fm,
                       jnp.where(lane == 5, -fl, 0.0))))
        qa_ref[h] = qa.astype(bf16)
        ka_ref[h] = ka.astype(bf16)


def _forget_cumsum(flog, bias, layer, *, batch, seq, tb):
    nsb = seq // tb
    t = batch * seq
    return pl.pallas_call(
        functools.partial(_fcum_body, tb=tb),
        grid=(batch, nsb),
        in_specs=[
            pl.BlockSpec((tb, LANE), lambda b, s: (b * nsb + s, 0)),
            pl.BlockSpec((None, 1, LANE), lambda b, s: (layer, 0, 0)),
        ],
        out_specs=[
            pl.BlockSpec((N_HEADS_B, tb, LANE), lambda b, s: (0, b * nsb + s, 0)),
            pl.BlockSpec((N_HEADS_B, tb, LANE), lambda b, s: (0, b * nsb + s, 0)),
        ],
        out_shape=[
            jax.ShapeDtypeStruct((N_HEADS_B, t, LANE), bf16),
            jax.ShapeDtypeStruct((N_HEADS_B, t, LANE), bf16),
        ],
        scratch_shapes=[pltpu.VMEM((1, LANE), f32)],
        compiler_params=_cparams(("parallel", "arbitrary"), 32),
        name="forget_cumsum",
    )(flog, bias)


def _attn_a_body(q_ref, k_ref, v_ref, tb_ref, o_ref):
    i = pl.program_id(1)
    first = jnp.maximum(i - (BAND_BLOCKS - 1), 0)
    cb0 = jnp.maximum((BAND_BLOCKS - 1) - i, 0)
    for h in range(N_HEADS_A):
        q = q_ref[h]
        s_blocks = []
        for c in range(BAND_BLOCKS):
            ks = pl.multiple_of((first + c) * QA_BLOCK, QA_BLOCK)
            kb = k_ref[h, pl.ds(ks, QA_BLOCK), :]
            s = lax.dot_general(q, kb, (((1,), (1,)), ((), ())), preferred_element_type=f32)
            s_blocks.append(s * SCALE + tb_ref[h, cb0 + c])
        m = s_blocks[0].max(axis=-1, keepdims=True)
        for s in s_blocks[1:]:
            m = jnp.maximum(m, s.max(axis=-1, keepdims=True))
        l = jnp.zeros_like(m)
        acc = jnp.zeros((QA_BLOCK, HEAD_DIM), f32)
        for c in range(BAND_BLOCKS):
            p = jnp.exp(s_blocks[c] - m)
            l = l + p.sum(axis=-1, keepdims=True)
            ks = pl.multiple_of((first + c) * QA_BLOCK, QA_BLOCK)
            vb = v_ref[h, pl.ds(ks, QA_BLOCK), :]
            acc = acc + jnp.dot(p.astype(bf16), vb, preferred_element_type=f32)
        o_ref[h] = (acc / l).astype(bf16)


def _attn_a(proj, tables, layer, *, batch, seq, ng):
    nq = seq // QA_BLOCK
    base = ng // N_HEADS_A
    t = batch * seq
    return pl.pallas_call(
        _attn_a_body,
        grid=(batch, nq),
        in_specs=[
            pl.BlockSpec((N_HEADS_A, QA_BLOCK, LANE), lambda b, i: (base, b * nq + i, 0)),
            pl.BlockSpec((N_HEADS_A, seq, LANE), lambda b, i: (base + 1, b, 0)),
            pl.BlockSpec((N_HEADS_A, seq, LANE), lambda b, i: (base + 2, b, 0)),
            _layer_spec(tables.shape, layer),
        ],
        out_specs=pl.BlockSpec((N_HEADS_A, QA_BLOCK, LANE), lambda b, i: (0, b * nq + i, 0)),
        out_shape=jax.ShapeDtypeStruct((N_HEADS_A, t, LANE), bf16),
        compiler_params=_cparams(("parallel", "arbitrary"), 48),
        name="attn_chunk",
    )(proj, proj, proj, tables)


def _attn_b_body(q_ref, k_ref, v_ref, qa_ref, ka_ref, o_ref, *, tq):
    i = pl.program_id(1)
    row = lax.broadcasted_iota(jnp.int32, (tq, tq), 0)
    col = lax.broadcasted_iota(jnp.int32, (tq, tq), 1)
    causal = col <= row

    for h in range(N_HEADS_B):
        qh = (q_ref[h].astype(f32) * (SCALE * LOG2E)).astype(bf16)
        qp = jnp.concatenate([qh, qa_ref[h]], axis=-1)

        def step(kb, carry, masked):
            m, l, acc = carry
            ks = pl.multiple_of(kb * tq, tq)
            kp = jnp.concatenate([k_ref[h, pl.ds(ks, tq), :], ka_ref[h, pl.ds(ks, tq), :]], axis=-1)
            s = lax.dot_general(qp, kp, (((1,), (1,)), ((), ())), preferred_element_type=f32)
            if masked:
                s = jnp.where(causal, s, NEG)
            m_new = jnp.maximum(m, s.max(axis=-1, keepdims=True))
            alpha = jnp.exp2(m - m_new)
            p = jnp.exp2(s - m_new)
            l = alpha * l + p.sum(axis=-1, keepdims=True)
            acc = alpha * acc + jnp.dot(p.astype(bf16), v_ref[h, pl.ds(ks, tq), :],
                                        preferred_element_type=f32)
            return m_new, l, acc

        init = (jnp.full((tq, 1), NEG, f32), jnp.zeros((tq, 1), f32), jnp.zeros((tq, HEAD_DIM), f32))
        carry = lax.fori_loop(0, i, lambda kb, c: step(kb, c, False), init)
        m, l, acc = step(i, carry, True)
        o_ref[h] = (acc / l).astype(bf16)


def _attn_b(proj, qaug, kaug, *, batch, seq, ng, tq):
    nq = seq // tq
    base = ng // N_HEADS_B + 3
    t = batch * seq
    return pl.pallas_call(
        functools.partial(_attn_b_body, tq=tq),
        grid=(batch, nq),
        in_specs=[
            pl.BlockSpec((N_HEADS_B, tq, LANE), lambda b, i: (base, b * nq + i, 0)),
            pl.BlockSpec((N_HEADS_B, seq, LANE), lambda b, i: (base + 1, b, 0)),
            pl.BlockSpec((N_HEADS_B, seq, LANE), lambda b, i: (base + 2, b, 0)),
            pl.BlockSpec((N_HEADS_B, tq, LANE), lambda b, i: (0, b * nq + i, 0)),
            pl.BlockSpec((N_HEADS_B, seq, LANE), lambda b, i: (0, b, 0)),
        ],
        out_specs=pl.BlockSpec((N_HEADS_B, tq, LANE), lambda b, i: (0, b * nq + i, 0)),
        out_shape=jax.ShapeDtypeStruct((N_HEADS_B, t, LANE), bf16),
        compiler_params=_cparams(("parallel", "arbitrary"), 56),
        name="attn_forget",
    )(proj, proj, proj, qaug, kaug)


def _attn_m_body(q_ref, k_ref, v_ref, o_ref):
    for h in range(N_HEADS_M):
        s = lax.dot_general(q_ref[h], k_ref[h], (((1,), (1,)), ((), ())), preferred_element_type=f32)
        s = s * SCALE
        m = s.max(axis=-1, keepdims=True)
        p = jnp.exp(s - m)
        l = p.sum(axis=-1, keepdims=True)
        acc = jnp.dot(p.astype(bf16), v_ref[h], preferred_element_type=f32)
        o_ref[h] = (acc / l).astype(bf16)


def _attn_m(proj, kv, *, batch, seq, n_mem, ng, tq):
    nq = seq // tq
    base = (ng + N_QKV_BLOCKS) // N_HEADS_M
    t = batch * seq
    return pl.pallas_call(
        _attn_m_body,
        grid=(batch, nq),
        in_specs=[
            pl.BlockSpec((N_HEADS_M, tq, LANE), lambda b, i: (base, b * nq + i, 0)),
            pl.BlockSpec((N_HEADS_M, n_mem, LANE), lambda b, i: (0, b, 0)),
            pl.BlockSpec((N_HEADS_M, n_mem, LANE), lambda b, i: (1, b, 0)),
        ],
        out_specs=pl.BlockSpec((N_HEADS_M, tq, LANE), lambda b, i: (0, b * nq + i, 0)),
        out_shape=jax.ShapeDtypeStruct((N_HEADS_M, t, LANE), bf16),
        compiler_params=_cparams(("parallel", "arbitrary"), 32),
        name="attn_mem",
    )(proj, kv, kv)


def _merge_body(h_ref, g_ref, oa_ref, ob_ref, om_ref, wa_ref, wb_ref, wm_ref, wo_ref,
                gn_ref, wr_ref, br_ref, hn_ref, xp_ref, lt_ref, *, d):
    ndb = d // LANE

    def heads(ref, n):
        return jnp.concatenate([ref[h] for h in range(n)], axis=-1)

    def gate(br):
        return jnp.concatenate([g_ref[br * ndb + c] for c in range(ndb)], axis=-1).astype(f32)

    o_a = jnp.dot(heads(oa_ref, N_HEADS_A), wa_ref[...], preferred_element_type=f32)
    merged = gate(0) * o_a
    o_b = jnp.dot(heads(ob_ref, N_HEADS_B), wb_ref[...], preferred_element_type=f32)
    merged = merged + gate(1) * o_b
    o_m = jnp.dot(heads(om_ref, N_HEADS_M), wm_ref[...], preferred_element_type=f32)
    merged = merged + gate(2) * o_m
    hn = h_ref[...] + jnp.dot(merged.astype(bf16), wo_ref[...], preferred_element_type=f32)
    hn_ref[...] = hn

    ms = jnp.mean(hn * hn, axis=-1, keepdims=True)
    xn = ((hn * lax.rsqrt(ms + EPS)) * gn_ref[...]).astype(bf16)
    logits = jnp.dot(xn, wr_ref[...], preferred_element_type=f32) + br_ref[...]
    lt_ref[...] = logits.T[:N_EXPERTS, :]
    xf = xn.astype(f32)
    lo = pltpu.bitcast(xf[:, :d // 2], jnp.uint32)
    hi = pltpu.bitcast(xf[:, d // 2:], jnp.uint32)
    xp_ref[...] = lax.shift_right_logical(lo, jnp.uint32(16)) | (hi & jnp.uint32(0xFFFF0000))


def _merge(h, proj, oa, ob, om, wa, wb, wm, wo, gn, wr, br, layer, *, tm):
    t, d = h.shape
    ng = 3 * d // LANE
    return pl.pallas_call(
        functools.partial(_merge_body, d=d),
        grid=(t // tm,),
        in_specs=[
            pl.BlockSpec((tm, d), lambda i: (i, 0)),
            pl.BlockSpec((ng, tm, LANE), lambda i: (0, i, 0)),
            pl.BlockSpec((N_HEADS_A, tm, LANE), lambda i: (0, i, 0)),
            pl.BlockSpec((N_HEADS_B, tm, LANE), lambda i: (0, i, 0)),
            pl.BlockSpec((N_HEADS_M, tm, LANE), lambda i: (0, i, 0)),
            _layer_spec(wa.shape, layer), _layer_spec(wb.shape, layer), _layer_spec(wm.shape, layer),
            _layer_spec(wo.shape, layer), _layer_spec(gn.shape, layer),
            _const_spec(wr.shape), _const_spec(br.shape),
        ],
        out_specs=[
            pl.BlockSpec((tm, d), lambda i: (i, 0)),
            pl.BlockSpec((tm, d // 2), lambda i: (i, 0)),
            pl.BlockSpec((N_EXPERTS, tm), lambda i: (0, i)),
        ],
        out_shape=[
            jax.ShapeDtypeStruct((t, d), f32),
            jax.ShapeDtypeStruct((t, d // 2), jnp.uint32),
            jax.ShapeDtypeStruct((N_EXPERTS, t), f32),
        ],
        compiler_params=_cparams(("parallel",), 56),
        name="merge_out_router",
    )(h, proj, oa, ob, om, wa, wb, wm, wo, gn, wr, br)


def _route_body(lt_ref, idx_ref, wt_ref, be_ref, nu_ref, run_ref, tot_ref, *, tn, nbp):
    phase = pl.program_id(0)
    i = pl.program_id(1)
    last = pl.num_programs(1) - 1

    @pl.when(i == 0)
    def _():
        run_ref[...] = jnp.zeros_like(run_ref)

    x = lt_ref[...]
    ex = jnp.exp(x - x.max(axis=0, keepdims=True))
    sc = ex / ex.sum(axis=0, keepdims=True)

    def top2(rows):
        m1 = functools.reduce(jnp.maximum, rows)
        i1 = jnp.full_like(m1, float(len(rows) - 1))
        for j in range(len(rows) - 2, -1, -1):
            i1 = jnp.where(rows[j] == m1, float(j), i1)
        rest = [jnp.where(i1 == float(j), -1.0, r) for j, r in enumerate(rows)]
        m2 = functools.reduce(jnp.maximum, rest)
        i2 = jnp.full_like(m2, float(len(rows) - 1))
        for j in range(len(rows) - 2, -1, -1):
            i2 = jnp.where(rest[j] == m2, float(j), i2)
        return m1, i1, m2, i2

    groups = [top2([sc[g * E_PER_GROUP + j:g * E_PER_GROUP + j + 1, :] for j in range(E_PER_GROUP)])
              for g in range(N_GROUPS)]
    gs = [g[0] + g[2] for g in groups]
    best = functools.reduce(jnp.maximum, gs)
    sel = [groups[N_GROUPS - 1][k] for k in range(4)]
    gi = jnp.full_like(best, float(N_GROUPS - 1))
    for g in range(N_GROUPS - 2, -1, -1):
        hit = gs[g] == best
        sel = [jnp.where(hit, groups[g][k], sel[k]) for k in range(4)]
        gi = jnp.where(hit, float(g), gi)
    m1, i1, m2, i2 = sel
    e0 = gi * E_PER_GROUP + i1
    e1 = gi * E_PER_GROUP + i2
    wsum = m1 + m2
    w0 = m1 / wsum
    w1 = m2 / wsum

    erow = lax.broadcasted_iota(jnp.int32, (N_EXPERTS, 1), 0).astype(f32)
    oh0 = (erow == e0).astype(f32)
    oh1 = (erow == e1).astype(f32)
    sel_mask = oh0 + oh1
    r = lax.broadcasted_iota(jnp.int32, (tn, tn), 0)
    c = lax.broadcasted_iota(jnp.int32, (tn, tn), 1)
    before = (r < c).astype(bf16)
    cnt = jnp.dot(sel_mask.astype(bf16), before, preferred_element_type=f32) + run_ref[:, 0:1]
    run_ref[...] = run_ref[...] + sel_mask.sum(axis=1, keepdims=True)

    @pl.when(jnp.logical_and(phase == 0, i == last))
    def _():
        tot_ref[...] = run_ref[...]

    @pl.when(phase == 1)
    def _():
        tot = tot_ref[:, 0:1]
        padded = jnp.floor((tot + (MOE_BLOCK - 1)) / MOE_BLOCK) * MOE_BLOCK
        start = jnp.zeros_like(padded)
        for e in range(N_EXPERTS - 1):
            start = start + jnp.where(erow > float(e), padded[e:e + 1, :], 0.0)
        slot = start + cnt
        d0 = (oh0 * slot).sum(axis=0, keepdims=True)
        d1 = (oh1 * slot).sum(axis=0, keepdims=True)
        r8 = lax.broadcasted_iota(jnp.int32, (8, 1), 0)
        rows = jnp.where(r8 == 0, e0, jnp.where(r8 == 1, e1, jnp.where(r8 == 2, d0, jnp.where(r8 == 3, d1, 0.0))))
        idx_ref[...] = rows.astype(jnp.int32)
        rl = lax.broadcasted_iota(jnp.int32, (LANE, 1), 0)
        wt_ref[...] = jnp.where(rl == 0, w0, jnp.where(rl == 1, w1, 0.0)).T
        end = start + padded
        blk = lax.broadcasted_iota(jnp.int32, (1, nbp), 1).astype(f32) * MOE_BLOCK
        be = (end <= blk).astype(f32).sum(axis=0, keepdims=True)
        be_ref[...] = jnp.minimum(be, N_EXPERTS - 1.0).astype(jnp.int32)
        nu_ref[...] = jnp.broadcast_to(end[N_EXPERTS - 1:, :] / MOE_BLOCK, (1, LANE)).astype(jnp.int32)


def _route(lt, *, tn, nbp):
    t = lt.shape[1]
    return pl.pallas_call(
        functools.partial(_route_body, tn=tn, nbp=nbp),
        grid=(2, t // tn),
        in_specs=[pl.BlockSpec((N_EXPERTS, tn), lambda p, i: (0, i))],
        out_specs=[
            pl.BlockSpec((8, tn), lambda p, i: (0, i * p)),
            pl.BlockSpec((tn, LANE), lambda p, i: (i * p, 0)),
            pl.BlockSpec((1, nbp), lambda p, i: (0, 0)),
            pl.BlockSpec((1, LANE), lambda p, i: (0, 0)),
        ],
        out_shape=[
            jax.ShapeDtypeStruct((8, t), jnp.int32),
            jax.ShapeDtypeStruct((t, LANE), f32),
            jax.ShapeDtypeStruct((1, nbp), jnp.int32),
            jax.ShapeDtypeStruct((1, LANE), jnp.int32),
        ],
        scratch_shapes=[pltpu.VMEM((N_EXPERTS, LANE), f32), pltpu.VMEM((N_EXPERTS, LANE), f32)],
        compiler_params=_cparams(("arbitrary", "arbitrary"), 32),
        name="route",
    )(lt)


def _dispatch_body(dest_ref, x_ref, prev_ref, xb_ref, sem, *, tc, t):
    del prev_ref
    base = pl.program_id(0) * tc

    def copy(r, d):
        return pltpu.make_async_copy(x_ref.at[pl.ds(r, 1), :], xb_ref.at[pl.ds(d, 1), :], sem)

    def issue(r, carry):
        copy(r, dest_ref[base + r]).start()
        copy(r, dest_ref[t + base + r]).start()
        return carry

    lax.fori_loop(0, tc, issue, 0)

    def drain(r, carry):
        copy(0, 0).wait()
        copy(0, 0).wait()
        return carry

    lax.fori_loop(0, tc, drain, 0)


def _dispatch(dest, xp, slots, *, tc):
    t, dh = xp.shape
    grid_spec = pltpu.PrefetchScalarGridSpec(
        num_scalar_prefetch=1,
        grid=(t // tc,),
        in_specs=[
            pl.BlockSpec((tc, dh), lambda i, dest: (i, 0)),
            pl.BlockSpec(memory_space=pl.ANY),
        ],
        out_specs=pl.BlockSpec(memory_space=pl.ANY),
        scratch_shapes=[pltpu.SemaphoreType.DMA],
    )
    return pl.pallas_call(
        functools.partial(_dispatch_body, tc=tc, t=t),
        grid_spec=grid_spec,
        out_shape=jax.ShapeDtypeStruct(slots.shape, slots.dtype),
        input_output_aliases={2: 0},
        compiler_params=_cparams(("arbitrary",), 32),
        name="dispatch",
    )(dest, xp, slots)


def _ffn_body(be_ref, nu_ref, x_ref, w1_ref, w3_ref, w2_ref, y_ref, *, dh):
    del be_ref
    b = pl.program_id(0)

    @pl.when(b < nu_ref[0])
    def _():
        u = x_ref[...]
        lo = pltpu.bitcast(lax.shift_left(u, jnp.uint32(16)), f32).astype(bf16)
        hi = pltpu.bitcast(u & jnp.uint32(0xFFFF0000), f32).astype(bf16)
        h1 = (jnp.dot(lo, w1_ref[:dh, :], preferred_element_type=f32)
              + jnp.dot(hi, w1_ref[dh:, :], preferred_element_type=f32))
        h3 = (jnp.dot(lo, w3_ref[:dh, :], preferred_element_type=f32)
              + jnp.dot(hi, w3_ref[dh:, :], preferred_element_type=f32))
        hid = (h1 / (1.0 + jnp.exp(-h1))) * h3
        y_ref[...] = jnp.dot(hid.astype(bf16), w2_ref[...], preferred_element_type=f32)

    @pl.when(b >= nu_ref[0])
    def _():
        y_ref[...] = jnp.zeros_like(y_ref)


def _ffn(block_e, n_used, xb, w1, w3, w2, layer):
    ns, dh = xb.shape
    _, _, d, f = w1.shape
    nb = ns // MOE_BLOCK
    grid_spec = pltpu.PrefetchScalarGridSpec(
        num_scalar_prefetch=2,
        grid=(nb,),
        in_specs=[
            pl.BlockSpec((MOE_BLOCK, dh), lambda b, be, nu: (b, 0)),
            pl.BlockSpec((None, None, d, f), lambda b, be, nu: (layer, be[b], 0, 0)),
            pl.BlockSpec((None, None, d, f), lambda b, be, nu: (layer, be[b], 0, 0)),
            pl.BlockSpec((None, None, f, d), lambda b, be, nu: (layer, be[b], 0, 0)),
        ],
        out_specs=pl.BlockSpec((MOE_BLOCK, d), lambda b, be, nu: (b, 0)),
    )
    return pl.pallas_call(
        functools.partial(_ffn_body, dh=dh),
        grid_spec=grid_spec,
        out_shape=jax.ShapeDtypeStruct((ns, d), f32),
        compiler_params=_cparams(("arbitrary",), 56),
        name="expert_ffn",
    )(block_e, n_used, xb, w1, w3, w2)


def _combine_body(dest_ref, h_ref, wt_ref, y_ref, o_ref, buf0, buf1, sem, *, tc, t):
    base = pl.program_id(0) * tc

    def copy(d, buf, r):
        return pltpu.make_async_copy(y_ref.at[pl.ds(d, 1), :], buf.at[pl.ds(r, 1), :], sem)

    def issue(r, carry):
        copy(dest_ref[base + r], buf0, r).start()
        copy(dest_ref[t + base + r], buf1, r).start()
        return carry

    lax.fori_loop(0, tc, issue, 0)

    def drain(r, carry):
        copy(0, buf0, 0).wait()
        copy(0, buf1, 0).wait()
        return carry

    lax.fori_loop(0, tc, drain, 0)
    w = wt_ref[...]
    o_ref[...] = h_ref[...] + (w[:, 0:1] * buf0[...] + w[:, 1:2] * buf1[...])


def _combine(dest, h, wt, y, *, tc):
    t, d = h.shape
    grid_spec = pltpu.PrefetchScalarGridSpec(
        num_scalar_prefetch=1,
        grid=(t // tc,),
        in_specs=[
            pl.BlockSpec((tc, d), lambda i, dest: (i, 0)),
            pl.BlockSpec((tc, LANE), lambda i, dest: (i, 0)),
            pl.BlockSpec(memory_space=pl.ANY),
        ],
        out_specs=pl.BlockSpec((tc, d), lambda i, dest: (i, 0)),
        scratch_shapes=[pltpu.VMEM((tc, d), f32), pltpu.VMEM((tc, d), f32), pltpu.SemaphoreType.DMA],
    )
    return pl.pallas_call(
        functools.partial(_combine_body, tc=tc, t=t),
        grid_spec=grid_spec,
        out_shape=jax.ShapeDtypeStruct((t, d), f32),
        compiler_params=_cparams(("arbitrary",), 48),
        name="combine",
    )(dest, h, wt, y)


def _final_norm_body(x_ref, g_ref, o_ref):
    x = x_ref[...]
    ms = jnp.mean(x * x, axis=-1, keepdims=True)
    o_ref[...] = (x * lax.rsqrt(ms + EPS)) * g_ref[...]


def _final_norm(x, g, *, tm):
    t, d = x.shape
    return pl.pallas_call(
        _final_norm_body,
        grid=(t // tm,),
        in_specs=[pl.BlockSpec((tm, d), lambda i: (i, 0)), pl.BlockSpec((1, d), lambda i: (0, 0))],
        out_specs=pl.BlockSpec((tm, d), lambda i: (i, 0)),
        out_shape=jax.ShapeDtypeStruct((t, d), f32),
        compiler_params=_cparams(("parallel",), 32),
        name="final_norm",
    )(x, g)


def _rel_bias_tables(rel_bias):
    depth, heads, _ = rel_bias.shape
    band0 = (BAND_BLOCKS - 1) * QA_BLOCK
    ncol = (2 * BAND_BLOCKS - 1) * QA_BLOCK
    n = ncol + QA_BLOCK
    rb = rel_bias.astype(f32)
    n_lo = band0 - REL_CLIP + QA_BLOCK - 1
    w = jnp.concatenate([
        jnp.broadcast_to(rb[..., :1], (depth, heads, n_lo)), rb,
        jnp.broadcast_to(rb[..., -1:], (depth, heads, n + 1 - n_lo - rb.shape[-1]))], axis=-1)
    skew = jnp.broadcast_to(w[:, :, None, :], (depth, heads, QA_BLOCK, n + 1))
    skew = skew.reshape(depth, heads, QA_BLOCK * (n + 1))[..., :QA_BLOCK * n]
    tab = skew.reshape(depth, heads, QA_BLOCK, n)[..., QA_BLOCK - 1:QA_BLOCK - 1 + ncol]
    row = jnp.arange(QA_BLOCK)[:, None]
    u = jnp.arange(ncol)[None, :]
    kc = u // CHUNK
    qc = row // CHUNK + BAND_CHUNKS
    valid = (kc >= qc - BAND_CHUNKS) & (kc <= qc)
    tab = jnp.where(valid, tab, NEG)
    return tab.reshape(depth, heads, QA_BLOCK, 2 * BAND_BLOCKS - 1, QA_BLOCK).transpose(0, 1, 3, 2, 4)


def _pad_lanes(w):
    pad = [(0, 0)] * (w.ndim - 1) + [(0, LANE - w.shape[-1])]
    return jnp.pad(w, pad)


def kernel(x, mem, norm_mix, w_in, b_forget, rel_bias, norm_mem, w_mem_kv, w_br_a, w_br_b, w_br_m,
           w_out, norm_ffn, w_router, b_router, w1, w3, w2, norm_final):
    batch, seq, d = x.shape
    n_mem = mem.shape[1]
    depth = w_in.shape[0]
    t = batch * seq
    ng = 3 * d // LANE
    assert ng % N_HEADS_A == 0 and (ng + N_QKV_BLOCKS) % N_HEADS_M == 0
    assert seq % 512 == 0 and n_mem % LANE == 0

    tn_proj = min(1024, math.gcd(3 * d, OFF_FB + W_M))
    tm_proj = min(1024, t)
    tq_b = 512
    tc = min(512, t)
    n_slots = (-(-(t * 2) // MOE_BLOCK) + N_EXPERTS) * MOE_BLOCK
    nb = n_slots // MOE_BLOCK
    nbp = -(-nb // LANE) * LANE

    w_main, w_f = _prep_w_in(w_in, tk=min(256, d))
    g_mix = norm_mix.reshape(depth, 1, d)
    g_mem = norm_mem.reshape(depth, 1, d)
    g_ffn = norm_ffn.reshape(depth, 1, d)
    bias_f = _pad_lanes(b_forget.astype(f32)).reshape(depth, 1, LANE)
    w_kv = w_mem_kv.astype(bf16)
    zeros_f = jnp.zeros((depth, d, LANE), bf16)
    tables = _rel_bias_tables(rel_bias)
    wa, wb, wm, wo = (w.astype(bf16) for w in (w_br_a, w_br_b, w_br_m, w_out))
    wr = _pad_lanes(w_router).astype(bf16)
    br = _pad_lanes(b_router.reshape(1, -1)).astype(f32)
    w1b, w3b, w2b = (w.astype(bf16) for w in (w1, w3, w2))

    h = x.reshape(t, d)
    memf = mem.reshape(batch * n_mem, d)
    xb = jnp.zeros((n_slots, d // 2), jnp.uint32)

    for l in range(depth):
        proj, flog = _norm_proj(h, g_mix, w_main, w_f, l, n_sig_cols=3 * d, tm=tm_proj, tn=tn_proj)
        qaug, kaug = _forget_cumsum(flog, bias_f, l, batch=batch, seq=seq, tb=tq_b)
        kv, _ = _norm_proj(memf, g_mem, w_kv, zeros_f, l, n_sig_cols=0,
                           tm=min(1024, batch * n_mem), tn=W_M)

        oa = _attn_a(proj, tables, l, batch=batch, seq=seq, ng=ng)
        ob = _attn_b(proj, qaug, kaug, batch=batch, seq=seq, ng=ng, tq=tq_b)
        om = _attn_m(proj, kv, batch=batch, seq=seq, n_mem=n_mem, ng=ng, tq=512)

        h, xp, lt = _merge(h, proj, oa, ob, om, wa, wb, wm, wo, g_ffn, wr, br, l, tm=256)

        idx, wt, block_e, n_used = _route(lt, tn=512, nbp=nbp)
        dest = idx[2:4].reshape(2 * t)
        xb = _dispatch(dest, xp, xb, tc=tc)
        yb = _ffn(block_e[0, :nb], n_used[0, :1], xb, w1b, w3b, w2b, l)
        h = _combine(dest, h, wt, yb, tc=tc)

    return _final_norm(h, norm_final.reshape(1, d), tm=min(512, t)).reshape(batch, seq, d)
```

```python
import functools
import math

import jax
import jax.numpy as jnp
from jax import lax
from jax.experimental import pallas as pl
from jax.experimental.pallas import tpu as pltpu

CHUNK = 64
HEAD_DIM = 128
N_HEADS_A = 6
N_HEADS_B = 6
N_HEADS_M = 4
BAND_CHUNKS = 8
REL_CLIP = 128
N_EXPERTS = 16
N_GROUPS = 4
E_PER_GROUP = 4
MOE_BLOCK = 256
EPS = 1e-6
SCALE = HEAD_DIM ** -0.5
LOG2E = 1.4426950408889634

LANE = 128
BF16_ROWS = 16
NEG = -1e30
QA_BLOCK = 4 * CHUNK
BAND_BLOCKS = BAND_CHUNKS * CHUNK // QA_BLOCK + 1
N_QKV_BLOCKS = 3 * N_HEADS_A + 3 * N_HEADS_B
W_A = N_HEADS_A * HEAD_DIM
W_B = N_HEADS_B * HEAD_DIM
W_M = N_HEADS_M * HEAD_DIM
OFF_FB = 3 * W_A + 3 * W_B
OFF_QM = OFF_FB + N_HEADS_B
OFF_G = OFF_QM + W_M

f32 = jnp.float32
bf16 = jnp.bfloat16


def _cparams(sem, vmem_mb):
    return pltpu.CompilerParams(dimension_semantics=sem, vmem_limit_bytes=vmem_mb * 1024 * 1024)


def _layer_spec(shape, layer):
    nd = len(shape) - 1
    return pl.BlockSpec((None,) + tuple(shape[1:]), lambda *_: (layer,) + (0,) * nd,
                        pipeline_mode=pl.Buffered(1))


def _const_spec(shape):
    nd = len(shape)
    return pl.BlockSpec(shape, lambda *_: (0,) * nd, pipeline_mode=pl.Buffered(1))


def _prep_body(w_ref, o_ref, f_ref, *, d):
    o_ref[:, :3 * d] = w_ref[:, OFF_G:OFF_G + 3 * d].astype(bf16)
    o_ref[:, 3 * d:3 * d + OFF_FB] = w_ref[:, :OFF_FB].astype(bf16)
    o_ref[:, 3 * d + OFF_FB:] = w_ref[:, OFF_QM:OFF_G].astype(bf16)
    lane = lax.broadcasted_iota(jnp.int32, (1, LANE), 1)
    f_ref[...] = jnp.where(lane < N_HEADS_B, w_ref[:, OFF_FB:OFF_FB + LANE], 0.0).astype(bf16)


def _prep_w_in(w_in, *, tk):
    depth, d, n_in = w_in.shape
    nm = 3 * d + OFF_FB + W_M
    return pl.pallas_call(
        functools.partial(_prep_body, d=d),
        grid=(depth, d // tk),
        in_specs=[pl.BlockSpec((None, tk, n_in), lambda l, k: (l, k, 0))],
        out_specs=[
            pl.BlockSpec((None, tk, nm), lambda l, k: (l, k, 0)),
            pl.BlockSpec((None, tk, LANE), lambda l, k: (l, k, 0)),
        ],
        out_shape=[
            jax.ShapeDtypeStruct((depth, d, nm), bf16),
            jax.ShapeDtypeStruct((depth, d, LANE), bf16),
        ],
        compiler_params=_cparams(("parallel", "parallel"), 48),
        name="prep_w_in",
    )(w_in)


def _proj_body(x_ref, g_ref, w_ref, wf_ref, o_ref, f_ref, xn_ref, *, n_sig, ncb):
    j = pl.program_id(1)

    @pl.when(j == 0)
    def _():
        x = x_ref[...]
        ms = jnp.mean(x * x, axis=-1, keepdims=True)
        xn = ((x * lax.rsqrt(ms + EPS)) * g_ref[...]).astype(bf16)
        xn_ref[...] = xn
        f_ref[...] = jnp.dot(xn, wf_ref[...], preferred_element_type=f32)

    acc = jnp.dot(xn_ref[...], w_ref[...], preferred_element_type=f32)
    for c in range(ncb):
        a = acc[:, c * LANE:(c + 1) * LANE]
        if n_sig > 0:
            a = jnp.where(j < n_sig, 1.0 / (1.0 + jnp.exp(-a)), a)
        o_ref[c] = a.astype(bf16)


def _norm_proj(x, g, w, wf, layer, *, n_sig_cols, tm, tn):
    m, d = x.shape
    n = w.shape[2]
    ncb = tn // LANE
    return pl.pallas_call(
        functools.partial(_proj_body, n_sig=n_sig_cols // tn, ncb=ncb),
        grid=(m // tm, n // tn),
        in_specs=[
            pl.BlockSpec((tm, d), lambda i, j: (i, 0)),
            pl.BlockSpec((None, 1, d), lambda i, j: (layer, 0, 0)),
            pl.BlockSpec((None, d, tn), lambda i, j: (layer, 0, j)),
            pl.BlockSpec((None, d, LANE), lambda i, j: (layer, 0, 0)),
        ],
        out_specs=[
            pl.BlockSpec((ncb, tm, LANE), lambda i, j: (j, i, 0)),
            pl.BlockSpec((tm, LANE), lambda i, j: (i, 0)),
        ],
        out_shape=[
            jax.ShapeDtypeStruct((n // LANE, m, LANE), bf16),
            jax.ShapeDtypeStruct((m, LANE), f32),
        ],
        scratch_shapes=[pltpu.VMEM((tm, d), bf16)],
        compiler_params=_cparams(("parallel", "arbitrary"), 56),
        name="norm_proj",
    )(x, g, w, wf)


def _split3(x):
    hi = x.astype(bf16)
    r1 = x - hi.astype(f32)
    mid = r1.astype(bf16)
    lo = (r1 - mid.astype(f32)).astype(bf16)
    return hi, mid, lo


def _fcum_body(fl_ref, b_ref, qa_ref, ka_ref, carry_ref, *, tb):
    @pl.when(pl.program_id(1) == 0)
    def _():
        carry_ref[...] = jnp.zeros_like(carry_ref)

    x = fl_ref[...] + b_ref[...]
    lf = jnp.minimum(x, 0.0) - jnp.log1p(jnp.exp(-jnp.abs(x)))
    hi, mid, lo = _split3(lf)
    row = lax.broadcasted_iota(jnp.int32, (tb, tb), 0)
    col = lax.broadcasted_iota(jnp.int32, (tb, tb), 1)
    tri = (col <= row).astype(bf16)
    c = (jnp.dot(tri, hi, preferred_element_type=f32)
         + jnp.dot(tri, mid, preferred_element_type=f32)
         + jnp.dot(tri, lo, preferred_element_type=f32))
    c = c + carry_ref[...]
    carry_ref[...] = c[tb - 1:tb, :]

    lane = lax.broadcasted_iota(jnp.int32, (1, LANE), 1)
    for h in range(N_HEADS_B):
        fh, fm, fl = (p.astype(f32) for p in _split3(c[:, h:h + 1] * LOG2E))
        qa = jnp.where(lane == 0, fh, jnp.where(lane == 1, fm, jnp.where(lane == 2, fl,
                       jnp.where(lane < 6, 1.0, 0.0))))
        ka = jnp.where(lane < 3, 1.0, jnp.where(lane == 3, -fh, jnp.where(lane == 4, -fm,
                       jnp.where(lane == 5, -fl, 0.0))))
        qa_ref[h] = qa.astype(bf16)
        ka_ref[h] = ka.astype(bf16)


def _forget_cumsum(flog, bias, layer, *, batch, seq, tb):
    nsb = seq // tb
    t = batch * seq
    return pl.pallas_call(
        functools.partial(_fcum_body, tb=tb),
        grid=(batch, nsb),
        in_specs=[
            pl.BlockSpec((tb, LANE), lambda b, s: (b * nsb + s, 0)),
            pl.BlockSpec((None, 1, LANE), lambda b, s: (layer, 0, 0)),
        ],
        out_specs=[
            pl.BlockSpec((N_HEADS_B, tb, LANE), lambda b, s: (0, b * nsb + s, 0)),
            pl.BlockSpec((N_HEADS_B, tb, LANE), lambda b, s: (0, b * nsb + s, 0)),
        ],
        out_shape=[
            jax.ShapeDtypeStruct((N_HEADS_B, t, LANE), bf16),
            jax.ShapeDtypeStruct((N_HEADS_B, t, LANE), bf16),
        ],
        scratch_shapes=[pltpu.VMEM((1, LANE), f32)],
        compiler_params=_cparams(("parallel", "arbitrary"), 32),
        name="forget_cumsum",
    )(flog, bias)


def _attn_a_body(q_ref, k_ref, v_ref, tb_ref, o_ref):
    i = pl.program_id(1)
    first = jnp.maximum(i - (BAND_BLOCKS - 1), 0)
    cb0 = jnp.maximum((BAND_BLOCKS - 1) - i, 0)
    for h in range(N_HEADS_A):
        q = q_ref[h]
        s_blocks = []
        for c in range(BAND_BLOCKS):
            ks = pl.multiple_of((first + c) * QA_BLOCK, QA_BLOCK)
            kb = k_ref[h, pl.ds(ks, QA_BLOCK), :]
            s = lax.dot_general(q, kb, (((1,), (1,)), ((), ())), preferred_element_type=f32)
            s_blocks.append(s * SCALE + tb_ref[h, cb0 + c])
        m = s_blocks[0].max(axis=-1, keepdims=True)
        for s in s_blocks[1:]:
            m = jnp.maximum(m, s.max(axis=-1, keepdims=True))
        l = jnp.zeros_like(m)
        acc = jnp.zeros((QA_BLOCK, HEAD_DIM), f32)
        for c in range(BAND_BLOCKS):
            p = jnp.exp(s_blocks[c] - m)
            l = l + p.sum(axis=-1, keepdims=True)
            ks = pl.multiple_of((first + c) * QA_BLOCK, QA_BLOCK)
            vb = v_ref[h, pl.ds(ks, QA_BLOCK), :]
            acc = acc + jnp.dot(p.astype(bf16), vb, preferred_element_type=f32)
        o_ref[h] = (acc / l).astype(bf16)


def _attn_a(proj, tables, layer, *, batch, seq, ng):
    nq = seq // QA_BLOCK
    base = ng // N_HEADS_A
    t = batch * seq
    return pl.pallas_call(
        _attn_a_body,
        grid=(batch, nq),
        in_specs=[
            pl.BlockSpec((N_HEADS_A, QA_BLOCK, LANE), lambda b, i: (base, b * nq + i, 0)),
            pl.BlockSpec((N_HEADS_A, seq, LANE), lambda b, i: (base + 1, b, 0)),
            pl.BlockSpec((N_HEADS_A, seq, LANE), lambda b, i: (base + 2, b, 0)),
            _layer_spec(tables.shape, layer),
        ],
        out_specs=pl.BlockSpec((N_HEADS_A, QA_BLOCK, LANE), lambda b, i: (0, b * nq + i, 0)),
        out_shape=jax.ShapeDtypeStruct((N_HEADS_A, t, LANE), bf16),
        compiler_params=_cparams(("parallel", "arbitrary"), 48),
        name="attn_chunk",
    )(proj, proj, proj, tables)


def _attn_b_body(q_ref, k_ref, v_ref, qa_ref, ka_ref, o_ref, *, tq):
    i = pl.program_id(1)
    row = lax.broadcasted_iota(jnp.int32, (tq, tq), 0)
    col = lax.broadcasted_iota(jnp.int32, (tq, tq), 1)
    causal = col <= row

    def one_head(h, qp, kb, carry, masked):
        m, l, acc = carry
        ks = pl.multiple_of(kb * tq, tq)
        kp = jnp.concatenate([k_ref[h, pl.ds(ks, tq), :], ka_ref[h, pl.ds(ks, tq), :]], axis=-1)
        s = lax.dot_general(qp, kp, (((1,), (1,)), ((), ())), preferred_element_type=f32)
        if masked:
            s = jnp.where(causal, s, NEG)
        m_new = jnp.maximum(m, s.max(axis=-1, keepdims=True))
        alpha = jnp.exp2(m - m_new)
        p = jnp.exp2(s - m_new)
        l = alpha * l + p.sum(axis=-1, keepdims=True)
        acc = alpha * acc + jnp.dot(p.astype(bf16), v_ref[h, pl.ds(ks, tq), :],
                                    preferred_element_type=f32)
        return m_new, l, acc

    for h0 in range(0, N_HEADS_B, 2):
        hs = (h0, h0 + 1)
        qps = [jnp.concatenate([(q_ref[h].astype(f32) * (SCALE * LOG2E)).astype(bf16), qa_ref[h]], axis=-1)
               for h in hs]

        def step(kb, carry, masked):
            return tuple(one_head(h, qp, kb, c, masked) for h, qp, c in zip(hs, qps, carry))

        init = (jnp.full((tq, 1), NEG, f32), jnp.zeros((tq, 1), f32), jnp.zeros((tq, HEAD_DIM), f32))
        carry = lax.fori_loop(0, i, lambda kb, c: step(kb, c, False), (init, init))
        for h, (m, l, acc) in zip(hs, step(i, carry, True)):
            o_ref[h] = (acc / l).astype(bf16)


def _attn_b(proj, qaug, kaug, *, batch, seq, ng, tq):
    nq = seq // tq
    base = ng // N_HEADS_B + 3
    t = batch * seq
    return pl.pallas_call(
        functools.partial(_attn_b_body, tq=tq),
        grid=(batch, nq),
        in_specs=[
            pl.BlockSpec((N_HEADS_B, tq, LANE), lambda b, i: (base, b * nq + i, 0)),
            pl.BlockSpec((N_HEADS_B, seq, LANE), lambda b, i: (base + 1, b, 0)),
            pl.BlockSpec((N_HEADS_B, seq, LANE), lambda b, i: (base + 2, b, 0)),
            pl.BlockSpec((N_HEADS_B, tq, LANE), lambda b, i: (0, b * nq + i, 0)),
            pl.BlockSpec((N_HEADS_B, seq, LANE), lambda b, i: (0, b, 0)),
        ],
        out_specs=pl.BlockSpec((N_HEADS_B, tq, LANE), lambda b, i: (0, b * nq + i, 0)),
        out_shape=jax.ShapeDtypeStruct((N_HEADS_B, t, LANE), bf16),
        compiler_params=_cparams(("parallel", "arbitrary"), 56),
        name="attn_forget",
    )(proj, proj, proj, qaug, kaug)


def _attn_m_body(q_ref, k_ref, v_ref, o_ref):
    for h in range(N_HEADS_M):
        s = lax.dot_general(q_ref[h], k_ref[h], (((1,), (1,)), ((), ())), preferred_element_type=f32)
        s = s * SCALE
        m = s.max(axis=-1, keepdims=True)
        p = jnp.exp(s - m)
        l = p.sum(axis=-1, keepdims=True)
        acc = jnp.dot(p.astype(bf16), v_ref[h], preferred_element_type=f32)
        o_ref[h] = (acc / l).astype(bf16)


def _attn_m(proj, kv, *, batch, seq, n_mem, ng, tq):
    nq = seq // tq
    base = (ng + N_QKV_BLOCKS) // N_HEADS_M
    t = batch * seq
    return pl.pallas_call(
        _attn_m_body,
        grid=(batch, nq),
        in_specs=[
            pl.BlockSpec((N_HEADS_M, tq, LANE), lambda b, i: (base, b * nq + i, 0)),
            pl.BlockSpec((N_HEADS_M, n_mem, LANE), lambda b, i: (0, b, 0)),
            pl.BlockSpec((N_HEADS_M, n_mem, LANE), lambda b, i: (1, b, 0)),
        ],
        out_specs=pl.BlockSpec((N_HEADS_M, tq, LANE), lambda b, i: (0, b * nq + i, 0)),
        out_shape=jax.ShapeDtypeStruct((N_HEADS_M, t, LANE), bf16),
        compiler_params=_cparams(("parallel", "arbitrary"), 32),
        name="attn_mem",
    )(proj, kv, kv)


def _merge_body(h_ref, g_ref, oa_ref, ob_ref, om_ref, wa_ref, wb_ref, wm_ref, wo_ref,
                gn_ref, wr_ref, br_ref, hn_ref, xp_ref, lt_ref, *, d):
    ndb = d // LANE

    def heads(ref, n):
        return jnp.concatenate([ref[h] for h in range(n)], axis=-1)

    def gate(br):
        return jnp.concatenate([g_ref[br * ndb + c] for c in range(ndb)], axis=-1).astype(f32)

    o_a = jnp.dot(heads(oa_ref, N_HEADS_A), wa_ref[...], preferred_element_type=f32)
    merged = gate(0) * o_a
    o_b = jnp.dot(heads(ob_ref, N_HEADS_B), wb_ref[...], preferred_element_type=f32)
    merged = merged + gate(1) * o_b
    o_m = jnp.dot(heads(om_ref, N_HEADS_M), wm_ref[...], preferred_element_type=f32)
    merged = merged + gate(2) * o_m
    hn = h_ref[...] + jnp.dot(merged.astype(bf16), wo_ref[...], preferred_element_type=f32)
    hn_ref[...] = hn

    ms = jnp.mean(hn * hn, axis=-1, keepdims=True)
    xn = ((hn * lax.rsqrt(ms + EPS)) * gn_ref[...]).astype(bf16)
    logits = jnp.dot(xn, wr_ref[...], preferred_element_type=f32) + br_ref[...]
    lt_ref[...] = logits.T[:N_EXPERTS, :]
    xf = xn.astype(f32)
    lo = pltpu.bitcast(xf[:, :d // 2], jnp.uint32)
    hi = pltpu.bitcast(xf[:, d // 2:], jnp.uint32)
    xp_ref[...] = lax.shift_right_logical(lo, jnp.uint32(16)) | (hi & jnp.uint32(0xFFFF0000))


def _merge(h, proj, oa, ob, om, wa, wb, wm, wo, gn, wr, br, layer, *, tm):
    t, d = h.shape
    ng = 3 * d // LANE
    return pl.pallas_call(
        functools.partial(_merge_body, d=d),
        grid=(t // tm,),
        in_specs=[
            pl.BlockSpec((tm, d), lambda i: (i, 0)),
            pl.BlockSpec((ng, tm, LANE), lambda i: (0, i, 0)),
            pl.BlockSpec((N_HEADS_A, tm, LANE), lambda i: (0, i, 0)),
            pl.BlockSpec((N_HEADS_B, tm, LANE), lambda i: (0, i, 0)),
            pl.BlockSpec((N_HEADS_M, tm, LANE), lambda i: (0, i, 0)),
            _layer_spec(wa.shape, layer), _layer_spec(wb.shape, layer), _layer_spec(wm.shape, layer),
            _layer_spec(wo.shape, layer), _layer_spec(gn.shape, layer),
            _const_spec(wr.shape), _const_spec(br.shape),
        ],
        out_specs=[
            pl.BlockSpec((tm, d), lambda i: (i, 0)),
            pl.BlockSpec((tm, d // 2), lambda i: (i, 0)),
            pl.BlockSpec((N_EXPERTS, tm), lambda i: (0, i)),
        ],
        out_shape=[
            jax.ShapeDtypeStruct((t, d), f32),
            jax.ShapeDtypeStruct((t, d // 2), jnp.uint32),
            jax.ShapeDtypeStruct((N_EXPERTS, t), f32),
        ],
        compiler_params=_cparams(("parallel",), 56),
        name="merge_out_router",
    )(h, proj, oa, ob, om, wa, wb, wm, wo, gn, wr, br)


def _route_body(lt_ref, idx_ref, wt_ref, be_ref, nu_ref, run_ref, tot_ref, *, tn, nbp):
    phase = pl.program_id(0)
    i = pl.program_id(1)
    last = pl.num_programs(1) - 1

    @pl.when(i == 0)
    def _():
        run_ref[...] = jnp.zeros_like(run_ref)

    x = lt_ref[...]
    ex = jnp.exp(x - x.max(axis=0, keepdims=True))
    sc = ex / ex.sum(axis=0, keepdims=True)

    def top2(rows):
        m1 = functools.reduce(jnp.maximum, rows)
        i1 = jnp.full_like(m1, float(len(rows) - 1))
        for j in range(len(rows) - 2, -1, -1):
            i1 = jnp.where(rows[j] == m1, float(j), i1)
        rest = [jnp.where(i1 == float(j), -1.0, r) for j, r in enumerate(rows)]
        m2 = functools.reduce(jnp.maximum, rest)
        i2 = jnp.full_like(m2, float(len(rows) - 1))
        for j in range(len(rows) - 2, -1, -1):
            i2 = jnp.where(rest[j] == m2, float(j), i2)
        return m1, i1, m2, i2

    groups = [top2([sc[g * E_PER_GROUP + j:g * E_PER_GROUP + j + 1, :] for j in range(E_PER_GROUP)])
              for g in range(N_GROUPS)]
    gs = [g[0] + g[2] for g in groups]
    best = functools.reduce(jnp.maximum, gs)
    sel = [groups[N_GROUPS - 1][k] for k in range(4)]
    gi = jnp.full_like(best, float(N_GROUPS - 1))
    for g in range(N_GROUPS - 2, -1, -1):
        hit = gs[g] == best
        sel = [jnp.where(hit, groups[g][k], sel[k]) for k in range(4)]
        gi = jnp.where(hit, float(g), gi)
    m1, i1, m2, i2 = sel
    e0 = gi * E_PER_GROUP + i1
    e1 = gi * E_PER_GROUP + i2
    wsum = m1 + m2
    w0 = m1 / wsum
    w1 = m2 / wsum

    erow = lax.broadcasted_iota(jnp.int32, (N_EXPERTS, 1), 0).astype(f32)
    oh0 = (erow == e0).astype(f32)
    oh1 = (erow == e1).astype(f32)
    sel_mask = oh0 + oh1
    r = lax.broadcasted_iota(jnp.int32, (tn, tn), 0)
    c = lax.broadcasted_iota(jnp.int32, (tn, tn), 1)
    before = (r < c).astype(bf16)
    cnt = jnp.dot(sel_mask.astype(bf16), before, preferred_element_type=f32) + run_ref[:, 0:1]
    run_ref[...] = run_ref[...] + sel_mask.sum(axis=1, keepdims=True)

    @pl.when(jnp.logical_and(phase == 0, i == last))
    def _():
        tot_ref[...] = run_ref[...]

    @pl.when(phase == 1)
    def _():
        tot = tot_ref[:, 0:1]
        padded = jnp.floor((tot + (MOE_BLOCK - 1)) / MOE_BLOCK) * MOE_BLOCK
        start = jnp.zeros_like(padded)
        for e in range(N_EXPERTS - 1):
            start = start + jnp.where(erow > float(e), padded[e:e + 1, :], 0.0)
        slot = start + cnt
        d0 = (oh0 * slot).sum(axis=0, keepdims=True)
        d1 = (oh1 * slot).sum(axis=0, keepdims=True)
        r8 = lax.broadcasted_iota(jnp.int32, (8, 1), 0)
        rows = jnp.where(r8 == 0, e0, jnp.where(r8 == 1, e1, jnp.where(r8 == 2, d0, jnp.where(r8 == 3, d1, 0.0))))
        idx_ref[...] = rows.astype(jnp.int32)
        rl = lax.broadcasted_iota(jnp.int32, (LANE, 1), 0)
        wt_ref[...] = jnp.where(rl == 0, w0, jnp.where(rl == 1, w1, 0.0)).T
        end = start + padded
        blk = lax.broadcasted_iota(jnp.int32, (1, nbp), 1).astype(f32) * MOE_BLOCK
        be = (end <= blk).astype(f32).sum(axis=0, keepdims=True)
        be_ref[...] = jnp.minimum(be, N_EXPERTS - 1.0).astype(jnp.int32)
        nu_ref[...] = jnp.broadcast_to(end[N_EXPERTS - 1:, :] / MOE_BLOCK, (1, LANE)).astype(jnp.int32)


def _route(lt, *, tn, nbp):
    t = lt.shape[1]
    return pl.pallas_call(
        functools.partial(_route_body, tn=tn, nbp=nbp),
        grid=(2, t // tn),
        in_specs=[pl.BlockSpec((N_EXPERTS, tn), lambda p, i: (0, i))],
        out_specs=[
            pl.BlockSpec((8, tn), lambda p, i: (0, i * p)),
            pl.BlockSpec((tn, LANE), lambda p, i: (i * p, 0)),
            pl.BlockSpec((1, nbp), lambda p, i: (0, 0)),
            pl.BlockSpec((1, LANE), lambda p, i: (0, 0)),
        ],
        out_shape=[
            jax.ShapeDtypeStruct((8, t), jnp.int32),
            jax.ShapeDtypeStruct((t, LANE), f32),
            jax.ShapeDtypeStruct((1, nbp), jnp.int32),
            jax.ShapeDtypeStruct((1, LANE), jnp.int32),
        ],
        scratch_shapes=[pltpu.VMEM((N_EXPERTS, LANE), f32), pltpu.VMEM((N_EXPERTS, LANE), f32)],
        compiler_params=_cparams(("arbitrary", "arbitrary"), 32),
        name="route",
    )(lt)


def _dispatch_body(dest_ref, x_ref, prev_ref, xb_ref, sem, *, tc, t):
    del prev_ref
    base = pl.program_id(0) * tc

    def copy(r, d):
        return pltpu.make_async_copy(x_ref.at[pl.ds(r, 1), :], xb_ref.at[pl.ds(d, 1), :], sem)

    def issue(r, carry):
        copy(r, dest_ref[base + r]).start()
        copy(r, dest_ref[t + base + r]).start()
        return carry

    lax.fori_loop(0, tc, issue, 0)

    def drain(r, carry):
        copy(0, 0).wait()
        copy(0, 0).wait()
        return carry

    lax.fori_loop(0, tc, drain, 0)


def _dispatch(dest, xp, slots, *, tc):
    t, dh = xp.shape
    grid_spec = pltpu.PrefetchScalarGridSpec(
        num_scalar_prefetch=1,
        grid=(t // tc,),
        in_specs=[
            pl.BlockSpec((tc, dh), lambda i, dest: (i, 0)),
            pl.BlockSpec(memory_space=pl.ANY),
        ],
        out_specs=pl.BlockSpec(memory_space=pl.ANY),
        scratch_shapes=[pltpu.SemaphoreType.DMA],
    )
    return pl.pallas_call(
        functools.partial(_dispatch_body, tc=tc, t=t),
        grid_spec=grid_spec,
        out_shape=jax.ShapeDtypeStruct(slots.shape, slots.dtype),
        input_output_aliases={2: 0},
        compiler_params=_cparams(("arbitrary",), 32),
        name="dispatch",
    )(dest, xp, slots)


def _ffn_body(be_ref, nu_ref, x_ref, w1_ref, w3_ref, w2_ref, y_ref, *, dh):
    del be_ref
    b = pl.program_id(0)

    @pl.when(b < nu_ref[0])
    def _():
        u = x_ref[...]
        lo = pltpu.bitcast(lax.shift_left(u, jnp.uint32(16)), f32).astype(bf16)
        hi = pltpu.bitcast(u & jnp.uint32(0xFFFF0000), f32).astype(bf16)
        h1 = (jnp.dot(lo, w1_ref[:dh, :], preferred_element_type=f32)
              + jnp.dot(hi, w1_ref[dh:, :], preferred_element_type=f32))
        h3 = (jnp.dot(lo, w3_ref[:dh, :], preferred_element_type=f32)
              + jnp.dot(hi, w3_ref[dh:, :], preferred_element_type=f32))
        hid = (h1 / (1.0 + jnp.exp(-h1))) * h3
        y_ref[...] = jnp.dot(hid.astype(bf16), w2_ref[...], preferred_element_type=f32)

    @pl.when(b >= nu_ref[0])
    def _():
        y_ref[...] = jnp.zeros_like(y_ref)


def _ffn(block_e, n_used, xb, w1, w3, w2, layer):
    ns, dh = xb.shape
    _, _, d, f = w1.shape
    nb = ns // MOE_BLOCK
    grid_spec = pltpu.PrefetchScalarGridSpec(
        num_scalar_prefetch=2,
        grid=(nb,),
        in_specs=[
            pl.BlockSpec((MOE_BLOCK, dh), lambda b, be, nu: (b, 0)),
            pl.BlockSpec((None, None, d, f), lambda b, be, nu: (layer, be[b], 0, 0)),
            pl.BlockSpec((None, None, d, f), lambda b, be, nu: (layer, be[b], 0, 0)),
            pl.BlockSpec((None, None, f, d), lambda b, be, nu: (layer, be[b], 0, 0)),
        ],
        out_specs=pl.BlockSpec((MOE_BLOCK, d), lambda b, be, nu: (b, 0)),
    )
    return pl.pallas_call(
        functools.partial(_ffn_body, dh=dh),
        grid_spec=grid_spec,
        out_shape=jax.ShapeDtypeStruct((ns, d), f32),
        compiler_params=_cparams(("arbitrary",), 56),
        name="expert_ffn",
    )(block_e, n_used, xb, w1, w3, w2)


def _combine_body(dest_ref, h_ref, wt_ref, y_ref, o_ref, buf0, buf1, sem, *, tc, t):
    base = pl.program_id(0) * tc

    def copy(d, buf, r):
        return pltpu.make_async_copy(y_ref.at[pl.ds(d, 1), :], buf.at[pl.ds(r, 1), :], sem)

    def issue(r, carry):
        copy(dest_ref[base + r], buf0, r).start()
        copy(dest_ref[t + base + r], buf1, r).start()
        return carry

    lax.fori_loop(0, tc, issue, 0)

    def drain(r, carry):
        copy(0, buf0, 0).wait()
        copy(0, buf1, 0).wait()
        return carry

    lax.fori_loop(0, tc, drain, 0)
    w = wt_ref[...]
    o_ref[...] = h_ref[...] + (w[:, 0:1] * buf0[...] + w[:, 1:2] * buf1[...])


def _combine(dest, h, wt, y, *, tc):
    t, d = h.shape
    grid_spec = pltpu.PrefetchScalarGridSpec(
        num_scalar_prefetch=1,
        grid=(t // tc,),
        in_specs=[
            pl.BlockSpec((tc, d), lambda i, dest: (i, 0)),
            pl.BlockSpec((tc, LANE), lambda i, dest: (i, 0)),
            pl.BlockSpec(memory_space=pl.ANY),
        ],
        out_specs=pl.BlockSpec((tc, d), lambda i, dest: (i, 0)),
        scratch_shapes=[pltpu.VMEM((tc, d), f32), pltpu.VMEM((tc, d), f32), pltpu.SemaphoreType.DMA],
    )
    return pl.pallas_call(
        functools.partial(_combine_body, tc=tc, t=t),
        grid_spec=grid_spec,
        out_shape=jax.ShapeDtypeStruct((t, d), f32),
        compiler_params=_cparams(("arbitrary",), 48),
        name="combine",
    )(dest, h, wt, y)


def _final_norm_body(x_ref, g_ref, o_ref):
    x = x_ref[...]
    ms = jnp.mean(x * x, axis=-1, keepdims=True)
    o_ref[...] = (x * lax.rsqrt(ms + EPS)) * g_ref[...]


def _final_norm(x, g, *, tm):
    t, d = x.shape
    return pl.pallas_call(
        _final_norm_body,
        grid=(t // tm,),
        in_specs=[pl.BlockSpec((tm, d), lambda i: (i, 0)), pl.BlockSpec((1, d), lambda i: (0, 0))],
        out_specs=pl.BlockSpec((tm, d), lambda i: (i, 0)),
        out_shape=jax.ShapeDtypeStruct((t, d), f32),
        compiler_params=_cparams(("parallel",), 32),
        name="final_norm",
    )(x, g)


def _rel_bias_tables(rel_bias):
    depth, heads, _ = rel_bias.shape
    band0 = (BAND_BLOCKS - 1) * QA_BLOCK
    ncol = (2 * BAND_BLOCKS - 1) * QA_BLOCK
    n = ncol + QA_BLOCK
    rb = rel_bias.astype(f32)
    n_lo = band0 - REL_CLIP + QA_BLOCK - 1
    w = jnp.concatenate([
        jnp.broadcast_to(rb[..., :1], (depth, heads, n_lo)), rb,
        jnp.broadcast_to(rb[..., -1:], (depth, heads, n + 1 - n_lo - rb.shape[-1]))], axis=-1)
    skew = jnp.broadcast_to(w[:, :, None, :], (depth, heads, QA_BLOCK, n + 1))
    skew = skew.reshape(depth, heads, QA_BLOCK * (n + 1))[..., :QA_BLOCK * n]
    tab = skew.reshape(depth, heads, QA_BLOCK, n)[..., QA_BLOCK - 1:QA_BLOCK - 1 + ncol]
    row = jnp.arange(QA_BLOCK)[:, None]
    u = jnp.arange(ncol)[None, :]
    kc = u // CHUNK
    qc = row // CHUNK + BAND_CHUNKS
    valid = (kc >= qc - BAND_CHUNKS) & (kc <= qc)
    tab = jnp.where(valid, tab, NEG)
    return tab.reshape(depth, heads, QA_BLOCK, 2 * BAND_BLOCKS - 1, QA_BLOCK).transpose(0, 1, 3, 2, 4)


def _pad_lanes(w):
    pad = [(0, 0)] * (w.ndim - 1) + [(0, LANE - w.shape[-1])]
    return jnp.pad(w, pad)


def kernel(x, mem, norm_mix, w_in, b_forget, rel_bias, norm_mem, w_mem_kv, w_br_a, w_br_b, w_br_m,
           w_out, norm_ffn, w_router, b_router, w1, w3, w2, norm_final):
    batch, seq, d = x.shape
    n_mem = mem.shape[1]
    depth = w_in.shape[0]
    t = batch * seq
    ng = 3 * d // LANE
    assert ng % N_HEADS_A == 0 and (ng + N_QKV_BLOCKS) % N_HEADS_M == 0
    assert seq % 512 == 0 and n_mem % LANE == 0

    tn_proj = min(1024, math.gcd(3 * d, OFF_FB + W_M))
    tm_proj = min(1024, t)
    tq_b = 512
    tc = min(512, t)
    n_slots = (-(-(t * 2) // MOE_BLOCK) + N_EXPERTS) * MOE_BLOCK
    nb = n_slots // MOE_BLOCK
    nbp = -(-nb // LANE) * LANE

    w_main, w_f = _prep_w_in(w_in, tk=min(256, d))
    g_mix = norm_mix.reshape(depth, 1, d)
    g_mem = norm_mem.reshape(depth, 1, d)
    g_ffn = norm_ffn.reshape(depth, 1, d)
    bias_f = _pad_lanes(b_forget.astype(f32)).reshape(depth, 1, LANE)
    w_kv = w_mem_kv.astype(bf16)
    zeros_f = jnp.zeros((depth, d, LANE), bf16)
    tables = _rel_bias_tables(rel_bias)
    wa, wb, wm, wo = (w.astype(bf16) for w in (w_br_a, w_br_b, w_br_m, w_out))
    wr = _pad_lanes(w_router).astype(bf16)
    br = _pad_lanes(b_router.reshape(1, -1)).astype(f32)
    w1b, w3b, w2b = (w.astype(bf16) for w in (w1, w3, w2))

    h = x.reshape(t, d)
    memf = mem.reshape(batch * n_mem, d)
    xb = jnp.zeros((n_slots, d // 2), jnp.uint32)

    for l in range(depth):
        proj, flog = _norm_proj(h, g_mix, w_main, w_f, l, n_sig_cols=3 * d, tm=tm_proj, tn=tn_proj)
        qaug, kaug = _forget_cumsum(flog, bias_f, l, batch=batch, seq=seq, tb=tq_b)
        kv, _ = _norm_proj(memf, g_mem, w_kv, zeros_f, l, n_sig_cols=0,
                           tm=min(1024, batch * n_mem), tn=W_M)

        oa = _attn_a(proj, tables, l, batch=batch, seq=seq, ng=ng)
        ob = _attn_b(proj, qaug, kaug, batch=batch, seq=seq, ng=ng, tq=tq_b)
        om = _attn_m(proj, kv, batch=batch, seq=seq, n_mem=n_mem, ng=ng, tq=512)

        h, xp, lt = _merge(h, proj, oa, ob, om, wa, wb, wm, wo, g_ffn, wr, br, l, tm=256)

        idx, wt, block_e, n_used = _route(lt, tn=512, nbp=nbp)
        dest = idx[2:4].reshape(2 * t)
        xb = _dispatch(dest, xp, xb, tc=tc)
        yb = _ffn(block_e[0, :nb], n_used[0, :1], xb, w1b, w3b, w2b, l)
        h = _combine(dest, h, wt, yb, tc=tc)

    return _final_norm(h, norm_final.reshape(1, d), tm=min(512, t)).reshape(batch, seq, d)
```

```python
import functools
import math

import jax
import jax.numpy as jnp
from jax import lax
from jax.experimental import pallas as pl
from jax.experimental.pallas import tpu as pltpu

CHUNK = 64
HEAD_DIM = 128
N_HEADS_A = 6
N_HEADS_B = 6
N_HEADS_M = 4
BAND_CHUNKS = 8
REL_CLIP = 128
N_EXPERTS = 16
N_GROUPS = 4
E_PER_GROUP = 4
MOE_BLOCK = 256
EPS = 1e-6
SCALE = HEAD_DIM ** -0.5
LOG2E = 1.4426950408889634

LANE = 128
BF16_ROWS = 16
NEG = -1e30
QA_BLOCK = 4 * CHUNK
BAND_BLOCKS = BAND_CHUNKS * CHUNK // QA_BLOCK + 1
N_QKV_BLOCKS = 3 * N_HEADS_A + 3 * N_HEADS_B
W_A = N_HEADS_A * HEAD_DIM
W_B = N_HEADS_B * HEAD_DIM
W_M = N_HEADS_M * HEAD_DIM
OFF_FB = 3 * W_A + 3 * W_B
OFF_QM = OFF_FB + N_HEADS_B
OFF_G = OFF_QM + W_M

f32 = jnp.float32
bf16 = jnp.bfloat16


def _cparams(sem, vmem_mb):
    return pltpu.CompilerParams(dimension_semantics=sem, vmem_limit_bytes=vmem_mb * 1024 * 1024)


def _layer_spec(shape, layer):
    nd = len(shape) - 1
    return pl.BlockSpec((None,) + tuple(shape[1:]), lambda *_: (layer,) + (0,) * nd,
                        pipeline_mode=pl.Buffered(1))


def _pack_halves(x):
    n = x.shape[1] // 2
    xf = x.astype(bf16).astype(f32)
    lo = pltpu.bitcast(xf[:, :n], jnp.uint32)
    hi = pltpu.bitcast(xf[:, n:], jnp.uint32)
    return lax.shift_right_logical(lo, jnp.uint32(16)) | (hi & jnp.uint32(0xFFFF0000))


def _unpack_halves(u):
    lo = pltpu.bitcast(lax.shift_left(u, jnp.uint32(16)), f32)
    hi = pltpu.bitcast(u & jnp.uint32(0xFFFF0000), f32)
    return lo, hi


def _const_spec(shape):
    nd = len(shape)
    return pl.BlockSpec(shape, lambda *_: (0,) * nd, pipeline_mode=pl.Buffered(1))


def _prep_body(w_ref, o_ref, f_ref, *, d):
    o_ref[:, :3 * d] = w_ref[:, OFF_G:OFF_G + 3 * d].astype(bf16)
    o_ref[:, 3 * d:3 * d + OFF_FB] = w_ref[:, :OFF_FB].astype(bf16)
    o_ref[:, 3 * d + OFF_FB:] = w_ref[:, OFF_QM:OFF_G].astype(bf16)
    lane = lax.broadcasted_iota(jnp.int32, (1, LANE), 1)
    f_ref[...] = jnp.where(lane < N_HEADS_B, w_ref[:, OFF_FB:OFF_FB + LANE], 0.0).astype(bf16)


def _prep_w_in(w_in, *, tk):
    depth, d, n_in = w_in.shape
    nm = 3 * d + OFF_FB + W_M
    return pl.pallas_call(
        functools.partial(_prep_body, d=d),
        grid=(depth, d // tk),
        in_specs=[pl.BlockSpec((None, tk, n_in), lambda l, k: (l, k, 0))],
        out_specs=[
            pl.BlockSpec((None, tk, nm), lambda l, k: (l, k, 0)),
            pl.BlockSpec((None, tk, LANE), lambda l, k: (l, k, 0)),
        ],
        out_shape=[
            jax.ShapeDtypeStruct((depth, d, nm), bf16),
            jax.ShapeDtypeStruct((depth, d, LANE), bf16),
        ],
        compiler_params=_cparams(("parallel", "parallel"), 48),
        name="prep_w_in",
    )(w_in)


def _proj_body(x_ref, g_ref, w_ref, wf_ref, o_ref, f_ref, xn_ref, *, n_sig, ncb):
    j = pl.program_id(1)

    @pl.when(j == 0)
    def _():
        x = x_ref[...]
        ms = jnp.mean(x * x, axis=-1, keepdims=True)
        xn = ((x * lax.rsqrt(ms + EPS)) * g_ref[...]).astype(bf16)
        xn_ref[...] = xn
        f_ref[...] = jnp.dot(xn, wf_ref[...], preferred_element_type=f32)

    acc = jnp.dot(xn_ref[...], w_ref[...], preferred_element_type=f32)
    for c in range(ncb):
        a = acc[:, c * LANE:(c + 1) * LANE]
        if n_sig > 0:
            a = jnp.where(j < n_sig, 1.0 / (1.0 + jnp.exp(-a)), a)
        o_ref[c] = a.astype(bf16)


def _norm_proj(x, g, w, wf, layer, *, n_sig_cols, tm, tn):
    m, d = x.shape
    n = w.shape[2]
    ncb = tn // LANE
    return pl.pallas_call(
        functools.partial(_proj_body, n_sig=n_sig_cols // tn, ncb=ncb),
        grid=(m // tm, n // tn),
        in_specs=[
            pl.BlockSpec((tm, d), lambda i, j: (i, 0)),
            pl.BlockSpec((None, 1, d), lambda i, j: (layer, 0, 0)),
            pl.BlockSpec((None, d, tn), lambda i, j: (layer, 0, j)),
            pl.BlockSpec((None, d, LANE), lambda i, j: (layer, 0, 0)),
        ],
        out_specs=[
            pl.BlockSpec((ncb, tm, LANE), lambda i, j: (j, i, 0)),
            pl.BlockSpec((tm, LANE), lambda i, j: (i, 0)),
        ],
        out_shape=[
            jax.ShapeDtypeStruct((n // LANE, m, LANE), bf16),
            jax.ShapeDtypeStruct((m, LANE), f32),
        ],
        scratch_shapes=[pltpu.VMEM((tm, d), bf16)],
        compiler_params=_cparams(("parallel", "arbitrary"), 56),
        name="norm_proj",
    )(x, g, w, wf)


def _split3(x):
    hi = x.astype(bf16)
    r1 = x - hi.astype(f32)
    mid = r1.astype(bf16)
    lo = (r1 - mid.astype(f32)).astype(bf16)
    return hi, mid, lo


def _fcum_body(fl_ref, b_ref, qa_ref, ka_ref, carry_ref, *, tb):
    @pl.when(pl.program_id(1) == 0)
    def _():
        carry_ref[...] = jnp.zeros_like(carry_ref)

    x = fl_ref[...] + b_ref[...]
    lf = jnp.minimum(x, 0.0) - jnp.log1p(jnp.exp(-jnp.abs(x)))
    hi, mid, lo = _split3(lf)
    row = lax.broadcasted_iota(jnp.int32, (tb, tb), 0)
    col = lax.broadcasted_iota(jnp.int32, (tb, tb), 1)
    tri = (col <= row).astype(bf16)
    c = (jnp.dot(tri, hi, preferred_element_type=f32)
         + jnp.dot(tri, mid, preferred_element_type=f32)
         + jnp.dot(tri, lo, preferred_element_type=f32))
    c = c + carry_ref[...]
    carry_ref[...] = c[tb - 1:tb, :]

    lane = lax.broadcasted_iota(jnp.int32, (1, LANE), 1)
    for h in range(N_HEADS_B):
        fh, fm, fl = (p.astype(f32) for p in _split3(c[:, h:h + 1] * LOG2E))
        qa = jnp.where(lane == 0, fh, jnp.where(lane == 1, fm, jnp.where(lane == 2, fl,
                       jnp.where(lane < 6, 1.0, 0.0))))
        ka = jnp.where(lane < 3, 1.0, jnp.where(lane == 3, -fh, jnp.where(lane == 4, -fm,
                       jnp.where(lane == 5, -fl, 0.0))))
        qa_ref[h] = qa.astype(bf16)
        ka_ref[h] = ka.astype(bf16)


def _forget_cumsum(flog, bias, layer, *, batch, seq, tb):
    nsb = seq // tb
    t = batch * seq
    return pl.pallas_call(
        functools.partial(_fcum_body, tb=tb),
        grid=(batch, nsb),
        in_specs=[
            pl.BlockSpec((tb, LANE), lambda b, s: (b * nsb + s, 0)),
            pl.BlockSpec((None, 1, LANE), lambda b, s: (layer, 0, 0)),
        ],
        out_specs=[
            pl.BlockSpec((N_HEADS_B, tb, LANE), lambda b, s: (0, b * nsb + s, 0)),
            pl.BlockSpec((N_HEADS_B, tb, LANE), lambda b, s: (0, b * nsb + s, 0)),
        ],
        out_shape=[
            jax.ShapeDtypeStruct((N_HEADS_B, t, LANE), bf16),
            jax.ShapeDtypeStruct((N_HEADS_B, t, LANE), bf16),
        ],
        scratch_shapes=[pltpu.VMEM((1, LANE), f32)],
        compiler_params=_cparams(("parallel", "arbitrary"), 32),
        name="forget_cumsum",
    )(flog, bias)


def _attn_a_body(q_ref, k_ref, v_ref, tb_ref, o_ref):
    i = pl.program_id(1)
    first = jnp.maximum(i - (BAND_BLOCKS - 1), 0)
    cb0 = jnp.maximum((BAND_BLOCKS - 1) - i, 0)
    for h in range(N_HEADS_A):
        q = q_ref[h]
        s_blocks = []
        for c in range(BAND_BLOCKS):
            ks = pl.multiple_of((first + c) * QA_BLOCK, QA_BLOCK)
            kb = k_ref[h, pl.ds(ks, QA_BLOCK), :]
            s = lax.dot_general(q, kb, (((1,), (1,)), ((), ())), preferred_element_type=f32)
            s_blocks.append(s * SCALE + tb_ref[h, cb0 + c])
        m = s_blocks[0].max(axis=-1, keepdims=True)
        for s in s_blocks[1:]:
            m = jnp.maximum(m, s.max(axis=-1, keepdims=True))
        l = jnp.zeros_like(m)
        acc = jnp.zeros((QA_BLOCK, HEAD_DIM), f32)
        for c in range(BAND_BLOCKS):
            p = jnp.exp(s_blocks[c] - m)
            l = l + p.sum(axis=-1, keepdims=True)
            ks = pl.multiple_of((first + c) * QA_BLOCK, QA_BLOCK)
            vb = v_ref[h, pl.ds(ks, QA_BLOCK), :]
            acc = acc + jnp.dot(p.astype(bf16), vb, preferred_element_type=f32)
        o_ref[h] = (acc / l).astype(bf16)


def _attn_a(proj, tables, layer, *, batch, seq, ng):
    nq = seq // QA_BLOCK
    base = ng // N_HEADS_A
    t = batch * seq
    return pl.pallas_call(
        _attn_a_body,
        grid=(batch, nq),
        in_specs=[
            pl.BlockSpec((N_HEADS_A, QA_BLOCK, LANE), lambda b, i: (base, b * nq + i, 0)),
            pl.BlockSpec((N_HEADS_A, seq, LANE), lambda b, i: (base + 1, b, 0)),
            pl.BlockSpec((N_HEADS_A, seq, LANE), lambda b, i: (base + 2, b, 0)),
            _layer_spec(tables.shape, layer),
        ],
        out_specs=pl.BlockSpec((N_HEADS_A, QA_BLOCK, LANE), lambda b, i: (0, b * nq + i, 0)),
        out_shape=jax.ShapeDtypeStruct((N_HEADS_A, t, LANE), bf16),
        compiler_params=_cparams(("parallel", "arbitrary"), 48),
        name="attn_chunk",
    )(proj, proj, proj, tables)


def _attn_b_body(q_ref, k_ref, v_ref, qa_ref, ka_ref, o_ref, *, tq):
    i = pl.program_id(1)
    row = lax.broadcasted_iota(jnp.int32, (tq, tq), 0)
    col = lax.broadcasted_iota(jnp.int32, (tq, tq), 1)
    causal = col <= row

    def one_head(h, qp, kb, carry, masked):
        m, l, acc = carry
        ks = pl.multiple_of(kb * tq, tq)
        kp = jnp.concatenate([k_ref[h, pl.ds(ks, tq), :], ka_ref[h, pl.ds(ks, tq), :]], axis=-1)
        s = lax.dot_general(qp, kp, (((1,), (1,)), ((), ())), preferred_element_type=f32)
        if masked:
            s = jnp.where(causal, s, NEG)
        m_new = jnp.maximum(m, s.max(axis=-1, keepdims=True))
        alpha = jnp.exp2(m - m_new)
        p = jnp.exp2(s - m_new)
        l = alpha * l + p.sum(axis=-1, keepdims=True)
        acc = alpha * acc + jnp.dot(p.astype(bf16), v_ref[h, pl.ds(ks, tq), :],
                                    preferred_element_type=f32)
        return m_new, l, acc

    for h0 in range(0, N_HEADS_B, 2):
        hs = (h0, h0 + 1)
        qps = [jnp.concatenate([(q_ref[h].astype(f32) * (SCALE * LOG2E)).astype(bf16), qa_ref[h]], axis=-1)
               for h in hs]

        def step(kb, carry, masked):
            return tuple(one_head(h, qp, kb, c, masked) for h, qp, c in zip(hs, qps, carry))

        init = (jnp.full((tq, 1), NEG, f32), jnp.zeros((tq, 1), f32), jnp.zeros((tq, HEAD_DIM), f32))
        carry = lax.fori_loop(0, i, lambda kb, c: step(kb, c, False), (init, init))
        for h, (m, l, acc) in zip(hs, step(i, carry, True)):
            o_ref[h] = (acc / l).astype(bf16)


def _attn_b(proj, qaug, kaug, *, batch, seq, ng, tq):
    nq = seq // tq
    base = ng // N_HEADS_B + 3
    t = batch * seq
    return pl.pallas_call(
        functools.partial(_attn_b_body, tq=tq),
        grid=(batch, nq),
        in_specs=[
            pl.BlockSpec((N_HEADS_B, tq, LANE), lambda b, i: (base, b * nq + i, 0)),
            pl.BlockSpec((N_HEADS_B, seq, LANE), lambda b, i: (base + 1, b, 0)),
            pl.BlockSpec((N_HEADS_B, seq, LANE), lambda b, i: (base + 2, b, 0)),
            pl.BlockSpec((N_HEADS_B, tq, LANE), lambda b, i: (0, b * nq + i, 0)),
            pl.BlockSpec((N_HEADS_B, seq, LANE), lambda b, i: (0, b, 0)),
        ],
        out_specs=pl.BlockSpec((N_HEADS_B, tq, LANE), lambda b, i: (0, b * nq + i, 0)),
        out_shape=jax.ShapeDtypeStruct((N_HEADS_B, t, LANE), bf16),
        compiler_params=_cparams(("parallel", "arbitrary"), 56),
        name="attn_forget",
    )(proj, proj, proj, qaug, kaug)


def _attn_m_body(q_ref, k_ref, v_ref, o_ref):
    for h in range(N_HEADS_M):
        s = lax.dot_general(q_ref[h], k_ref[h], (((1,), (1,)), ((), ())), preferred_element_type=f32)
        s = s * SCALE
        m = s.max(axis=-1, keepdims=True)
        p = jnp.exp(s - m)
        l = p.sum(axis=-1, keepdims=True)
        acc = jnp.dot(p.astype(bf16), v_ref[h], preferred_element_type=f32)
        o_ref[h] = (acc / l).astype(bf16)


def _attn_m(proj, kv, *, batch, seq, n_mem, ng, tq):
    nq = seq // tq
    base = (ng + N_QKV_BLOCKS) // N_HEADS_M
    t = batch * seq
    return pl.pallas_call(
        _attn_m_body,
        grid=(batch, nq),
        in_specs=[
            pl.BlockSpec((N_HEADS_M, tq, LANE), lambda b, i: (base, b * nq + i, 0)),
            pl.BlockSpec((N_HEADS_M, n_mem, LANE), lambda b, i: (0, b, 0)),
            pl.BlockSpec((N_HEADS_M, n_mem, LANE), lambda b, i: (1, b, 0)),
        ],
        out_specs=pl.BlockSpec((N_HEADS_M, tq, LANE), lambda b, i: (0, b * nq + i, 0)),
        out_shape=jax.ShapeDtypeStruct((N_HEADS_M, t, LANE), bf16),
        compiler_params=_cparams(("parallel", "arbitrary"), 32),
        name="attn_mem",
    )(proj, kv, kv)


def _merge_body(h_ref, g_ref, oa_ref, ob_ref, om_ref, wa_ref, wb_ref, wm_ref, wo_ref,
                gn_ref, wr_ref, br_ref, hn_ref, xp_ref, lt_ref, *, d):
    ndb = d // LANE

    def heads(ref, n):
        return jnp.concatenate([ref[h] for h in range(n)], axis=-1)

    def gate(br):
        return jnp.concatenate([g_ref[br * ndb + c] for c in range(ndb)], axis=-1).astype(f32)

    o_a = jnp.dot(heads(oa_ref, N_HEADS_A), wa_ref[...], preferred_element_type=f32)
    merged = gate(0) * o_a
    o_b = jnp.dot(heads(ob_ref, N_HEADS_B), wb_ref[...], preferred_element_type=f32)
    merged = merged + gate(1) * o_b
    o_m = jnp.dot(heads(om_ref, N_HEADS_M), wm_ref[...], preferred_element_type=f32)
    merged = merged + gate(2) * o_m
    hn = h_ref[...] + jnp.dot(merged.astype(bf16), wo_ref[...], preferred_element_type=f32)
    hn_ref[...] = hn

    ms = jnp.mean(hn * hn, axis=-1, keepdims=True)
    xn = ((hn * lax.rsqrt(ms + EPS)) * gn_ref[...]).astype(bf16)
    logits = jnp.dot(xn, wr_ref[...], preferred_element_type=f32) + br_ref[...]
    lt_ref[...] = logits.T[:N_EXPERTS, :]
    xp_ref[...] = _pack_halves(xn)


def _merge(h, proj, oa, ob, om, wa, wb, wm, wo, gn, wr, br, layer, *, tm):
    t, d = h.shape
    ng = 3 * d // LANE
    return pl.pallas_call(
        functools.partial(_merge_body, d=d),
        grid=(t // tm,),
        in_specs=[
            pl.BlockSpec((tm, d), lambda i: (i, 0)),
            pl.BlockSpec((ng, tm, LANE), lambda i: (0, i, 0)),
            pl.BlockSpec((N_HEADS_A, tm, LANE), lambda i: (0, i, 0)),
            pl.BlockSpec((N_HEADS_B, tm, LANE), lambda i: (0, i, 0)),
            pl.BlockSpec((N_HEADS_M, tm, LANE), lambda i: (0, i, 0)),
            _layer_spec(wa.shape, layer), _layer_spec(wb.shape, layer), _layer_spec(wm.shape, layer),
            _layer_spec(wo.shape, layer), _layer_spec(gn.shape, layer),
            _const_spec(wr.shape), _const_spec(br.shape),
        ],
        out_specs=[
            pl.BlockSpec((tm, d), lambda i: (i, 0)),
            pl.BlockSpec((tm, d // 2), lambda i: (i, 0)),
            pl.BlockSpec((N_EXPERTS, tm), lambda i: (0, i)),
        ],
        out_shape=[
            jax.ShapeDtypeStruct((t, d), f32),
            jax.ShapeDtypeStruct((t, d // 2), jnp.uint32),
            jax.ShapeDtypeStruct((N_EXPERTS, t), f32),
        ],
        compiler_params=_cparams(("parallel",), 56),
        name="merge_out_router",
    )(h, proj, oa, ob, om, wa, wb, wm, wo, gn, wr, br)


def _route_body(lt_ref, idx_ref, wt_ref, be_ref, nu_ref, run_ref, tot_ref, *, tn, nbp):
    phase = pl.program_id(0)
    i = pl.program_id(1)
    last = pl.num_programs(1) - 1

    @pl.when(i == 0)
    def _():
        run_ref[...] = jnp.zeros_like(run_ref)

    x = lt_ref[...]
    ex = jnp.exp(x - x.max(axis=0, keepdims=True))
    sc = ex / ex.sum(axis=0, keepdims=True)

    def top2(rows):
        m1 = functools.reduce(jnp.maximum, rows)
        i1 = jnp.full_like(m1, float(len(rows) - 1))
        for j in range(len(rows) - 2, -1, -1):
            i1 = jnp.where(rows[j] == m1, float(j), i1)
        rest = [jnp.where(i1 == float(j), -1.0, r) for j, r in enumerate(rows)]
        m2 = functools.reduce(jnp.maximum, rest)
        i2 = jnp.full_like(m2, float(len(rows) - 1))
        for j in range(len(rows) - 2, -1, -1):
            i2 = jnp.where(rest[j] == m2, float(j), i2)
        return m1, i1, m2, i2

    groups = [top2([sc[g * E_PER_GROUP + j:g * E_PER_GROUP + j + 1, :] for j in range(E_PER_GROUP)])
              for g in range(N_GROUPS)]
    gs = [g[0] + g[2] for g in groups]
    best = functools.reduce(jnp.maximum, gs)
    sel = [groups[N_GROUPS - 1][k] for k in range(4)]
    gi = jnp.full_like(best, float(N_GROUPS - 1))
    for g in range(N_GROUPS - 2, -1, -1):
        hit = gs[g] == best
        sel = [jnp.where(hit, groups[g][k], sel[k]) for k in range(4)]
        gi = jnp.where(hit, float(g), gi)
    m1, i1, m2, i2 = sel
    e0 = gi * E_PER_GROUP + i1
    e1 = gi * E_PER_GROUP + i2
    wsum = m1 + m2
    w0 = m1 / wsum
    w1 = m2 / wsum

    erow = lax.broadcasted_iota(jnp.int32, (N_EXPERTS, 1), 0).astype(f32)
    oh0 = (erow == e0).astype(f32)
    oh1 = (erow == e1).astype(f32)
    sel_mask = oh0 + oh1
    r = lax.broadcasted_iota(jnp.int32, (tn, tn), 0)
    c = lax.broadcasted_iota(jnp.int32, (tn, tn), 1)
    before = (r < c).astype(bf16)
    cnt = jnp.dot(sel_mask.astype(bf16), before, preferred_element_type=f32) + run_ref[:, 0:1]
    run_ref[...] = run_ref[...] + sel_mask.sum(axis=1, keepdims=True)

    @pl.when(jnp.logical_and(phase == 0, i == last))
    def _():
        tot_ref[...] = run_ref[...]

    @pl.when(phase == 1)
    def _():
        tot = tot_ref[:, 0:1]
        padded = jnp.floor((tot + (MOE_BLOCK - 1)) / MOE_BLOCK) * MOE_BLOCK
        start = jnp.zeros_like(padded)
        for e in range(N_EXPERTS - 1):
            start = start + jnp.where(erow > float(e), padded[e:e + 1, :], 0.0)
        slot = start + cnt
        d0 = (oh0 * slot).sum(axis=0, keepdims=True)
        d1 = (oh1 * slot).sum(axis=0, keepdims=True)
        r8 = lax.broadcasted_iota(jnp.int32, (8, 1), 0)
        rows = jnp.where(r8 == 0, e0, jnp.where(r8 == 1, e1, jnp.where(r8 == 2, d0, jnp.where(r8 == 3, d1, 0.0))))
        idx_ref[...] = rows.astype(jnp.int32)
        rl = lax.broadcasted_iota(jnp.int32, (LANE, 1), 0)
        wt_ref[...] = jnp.where(rl == 0, w0, jnp.where(rl == 1, w1, 0.0)).T
        end = start + padded
        blk = lax.broadcasted_iota(jnp.int32, (1, nbp), 1).astype(f32) * MOE_BLOCK
        be = (end <= blk).astype(f32).sum(axis=0, keepdims=True)
        be_ref[...] = jnp.minimum(be, N_EXPERTS - 1.0).astype(jnp.int32)
        nu_ref[...] = jnp.broadcast_to(end[N_EXPERTS - 1:, :] / MOE_BLOCK, (1, LANE)).astype(jnp.int32)


def _route(lt, *, tn, nbp):
    t = lt.shape[1]
    return pl.pallas_call(
        functools.partial(_route_body, tn=tn, nbp=nbp),
        grid=(2, t // tn),
        in_specs=[pl.BlockSpec((N_EXPERTS, tn), lambda p, i: (0, i))],
        out_specs=[
            pl.BlockSpec((8, tn), lambda p, i: (0, i * p)),
            pl.BlockSpec((tn, LANE), lambda p, i: (i * p, 0)),
            pl.BlockSpec((1, nbp), lambda p, i: (0, 0)),
            pl.BlockSpec((1, LANE), lambda p, i: (0, 0)),
        ],
        out_shape=[
            jax.ShapeDtypeStruct((8, t), jnp.int32),
            jax.ShapeDtypeStruct((t, LANE), f32),
            jax.ShapeDtypeStruct((1, nbp), jnp.int32),
            jax.ShapeDtypeStruct((1, LANE), jnp.int32),
        ],
        scratch_shapes=[pltpu.VMEM((N_EXPERTS, LANE), f32), pltpu.VMEM((N_EXPERTS, LANE), f32)],
        compiler_params=_cparams(("arbitrary", "arbitrary"), 32),
        name="route",
    )(lt)


def _dispatch_body(dest_ref, x_ref, prev_ref, xb_ref, sem, *, tc, t):
    del prev_ref
    base = pl.program_id(0) * tc

    def copy(r, d):
        return pltpu.make_async_copy(x_ref.at[pl.ds(r, 1), :], xb_ref.at[pl.ds(d, 1), :], sem)

    def issue(r, carry):
        copy(r, dest_ref[base + r]).start()
        copy(r, dest_ref[t + base + r]).start()
        return carry

    lax.fori_loop(0, tc, issue, 0, unroll=8)

    def drain(r, carry):
        copy(0, 0).wait()
        copy(0, 0).wait()
        return carry

    lax.fori_loop(0, tc, drain, 0, unroll=8)


def _dispatch(dest, xp, slots, *, tc):
    t, dh = xp.shape
    grid_spec = pltpu.PrefetchScalarGridSpec(
        num_scalar_prefetch=1,
        grid=(t // tc,),
        in_specs=[
            pl.BlockSpec((tc, dh), lambda i, dest: (i, 0)),
            pl.BlockSpec(memory_space=pl.ANY),
        ],
        out_specs=pl.BlockSpec(memory_space=pl.ANY),
        scratch_shapes=[pltpu.SemaphoreType.DMA],
    )
    return pl.pallas_call(
        functools.partial(_dispatch_body, tc=tc, t=t),
        grid_spec=grid_spec,
        out_shape=jax.ShapeDtypeStruct(slots.shape, slots.dtype),
        input_output_aliases={2: 0},
        compiler_params=_cparams(("arbitrary",), 32),
        name="dispatch",
    )(dest, xp, slots)


def _ffn_body(be_ref, nu_ref, x_ref, w1_ref, w3_ref, w2_ref, y_ref, *, dh):
    del be_ref
    b = pl.program_id(0)

    @pl.when(b < nu_ref[0])
    def _():
        lo, hi = (p.astype(bf16) for p in _unpack_halves(x_ref[...]))
        h1 = (jnp.dot(lo, w1_ref[:dh, :], preferred_element_type=f32)
              + jnp.dot(hi, w1_ref[dh:, :], preferred_element_type=f32))
        h3 = (jnp.dot(lo, w3_ref[:dh, :], preferred_element_type=f32)
              + jnp.dot(hi, w3_ref[dh:, :], preferred_element_type=f32))
        hid = (h1 / (1.0 + jnp.exp(-h1))) * h3
        y_ref[...] = _pack_halves(jnp.dot(hid.astype(bf16), w2_ref[...], preferred_element_type=f32))

    @pl.when(b >= nu_ref[0])
    def _():
        y_ref[...] = jnp.zeros_like(y_ref)


def _ffn(block_e, n_used, xb, w1, w3, w2, layer):
    ns, dh = xb.shape
    _, _, d, f = w1.shape
    nb = ns // MOE_BLOCK
    grid_spec = pltpu.PrefetchScalarGridSpec(
        num_scalar_prefetch=2,
        grid=(nb,),
        in_specs=[
            pl.BlockSpec((MOE_BLOCK, dh), lambda b, be, nu: (b, 0)),
            pl.BlockSpec((None, None, d, f), lambda b, be, nu: (layer, be[b], 0, 0)),
            pl.BlockSpec((None, None, d, f), lambda b, be, nu: (layer, be[b], 0, 0)),
            pl.BlockSpec((None, None, f, d), lambda b, be, nu: (layer, be[b], 0, 0)),
        ],
        out_specs=pl.BlockSpec((MOE_BLOCK, dh), lambda b, be, nu: (b, 0)),
    )
    return pl.pallas_call(
        functools.partial(_ffn_body, dh=dh),
        grid_spec=grid_spec,
        out_shape=jax.ShapeDtypeStruct((ns, dh), jnp.uint32),
        compiler_params=_cparams(("arbitrary",), 56),
        name="expert_ffn",
    )(block_e, n_used, xb, w1, w3, w2)


def _combine_body(dest_ref, h_ref, wt_ref, g_ref, y_ref, o_ref, buf0, buf1, sem, *, tc, t, final):
    base = pl.program_id(0) * tc

    def copy(d, buf, r):
        return pltpu.make_async_copy(y_ref.at[pl.ds(d, 1), :], buf.at[pl.ds(r, 1), :], sem)

    def issue(r, carry):
        copy(dest_ref[base + r], buf0, r).start()
        copy(dest_ref[t + base + r], buf1, r).start()
        return carry

    lax.fori_loop(0, tc, issue, 0, unroll=8)

    def drain(r, carry):
        copy(0, buf0, 0).wait()
        copy(0, buf1, 0).wait()
        return carry

    lax.fori_loop(0, tc, drain, 0, unroll=8)
    w = wt_ref[...]
    dh = buf0.shape[1]
    lo0, hi0 = _unpack_halves(buf0[...])
    lo1, hi1 = _unpack_halves(buf1[...])
    out_lo = h_ref[:, :dh] + (w[:, 0:1] * lo0 + w[:, 1:2] * lo1)
    out_hi = h_ref[:, dh:] + (w[:, 0:1] * hi0 + w[:, 1:2] * hi1)
    if final:
        ss = (out_lo * out_lo).sum(axis=-1, keepdims=True) + (out_hi * out_hi).sum(axis=-1, keepdims=True)
        inv = lax.rsqrt(ss / (2 * dh) + EPS)
        out_lo = (out_lo * inv) * g_ref[:, :dh]
        out_hi = (out_hi * inv) * g_ref[:, dh:]
    o_ref[:, :dh] = out_lo
    o_ref[:, dh:] = out_hi


def _combine(dest, h, wt, g, y, *, tc, final):
    t, d = h.shape
    grid_spec = pltpu.PrefetchScalarGridSpec(
        num_scalar_prefetch=1,
        grid=(t // tc,),
        in_specs=[
            pl.BlockSpec((tc, d), lambda i, dest: (i, 0)),
            pl.BlockSpec((tc, LANE), lambda i, dest: (i, 0)),
            pl.BlockSpec((1, d), lambda i, dest: (0, 0)),
            pl.BlockSpec(memory_space=pl.ANY),
        ],
        out_specs=pl.BlockSpec((tc, d), lambda i, dest: (i, 0)),
        scratch_shapes=[pltpu.VMEM((tc, d // 2), jnp.uint32), pltpu.VMEM((tc, d // 2), jnp.uint32),
                        pltpu.SemaphoreType.DMA],
    )
    return pl.pallas_call(
        functools.partial(_combine_body, tc=tc, t=t, final=final),
        grid_spec=grid_spec,
        out_shape=jax.ShapeDtypeStruct((t, d), f32),
        compiler_params=_cparams(("arbitrary",), 48),
        name="combine",
    )(dest, h, wt, g, y)


def _rel_bias_tables(rel_bias):
    depth, heads, _ = rel_bias.shape
    band0 = (BAND_BLOCKS - 1) * QA_BLOCK
    ncol = (2 * BAND_BLOCKS - 1) * QA_BLOCK
    n = ncol + QA_BLOCK
    rb = rel_bias.astype(f32)
    n_lo = band0 - REL_CLIP + QA_BLOCK - 1
    w = jnp.concatenate([
        jnp.broadcast_to(rb[..., :1], (depth, heads, n_lo)), rb,
        jnp.broadcast_to(rb[..., -1:], (depth, heads, n + 1 - n_lo - rb.shape[-1]))], axis=-1)
    skew = jnp.broadcast_to(w[:, :, None, :], (depth, heads, QA_BLOCK, n + 1))
    skew = skew.reshape(depth, heads, QA_BLOCK * (n + 1))[..., :QA_BLOCK * n]
    tab = skew.reshape(depth, heads, QA_BLOCK, n)[..., QA_BLOCK - 1:QA_BLOCK - 1 + ncol]
    row = jnp.arange(QA_BLOCK)[:, None]
    u = jnp.arange(ncol)[None, :]
    kc = u // CHUNK
    qc = row // CHUNK + BAND_CHUNKS
    valid = (kc >= qc - BAND_CHUNKS) & (kc <= qc)
    tab = jnp.where(valid, tab, NEG)
    return tab.reshape(depth, heads, QA_BLOCK, 2 * BAND_BLOCKS - 1, QA_BLOCK).transpose(0, 1, 3, 2, 4)


def _pad_lanes(w):
    pad = [(0, 0)] * (w.ndim - 1) + [(0, LANE - w.shape[-1])]
    return jnp.pad(w, pad)


def kernel(x, mem, norm_mix, w_in, b_forget, rel_bias, norm_mem, w_mem_kv, w_br_a, w_br_b, w_br_m,
           w_out, norm_ffn, w_router, b_router, w1, w3, w2, norm_final):
    batch, seq, d = x.shape
    n_mem = mem.shape[1]
    depth = w_in.shape[0]
    t = batch * seq
    ng = 3 * d // LANE
    assert ng % N_HEADS_A == 0 and (ng + N_QKV_BLOCKS) % N_HEADS_M == 0
    assert seq % 512 == 0 and n_mem % LANE == 0

    tn_proj = min(1024, math.gcd(3 * d, OFF_FB + W_M))
    tm_proj = min(1024, t)
    tq_b = 512
    tc = min(512, t)
    n_slots = (-(-(t * 2) // MOE_BLOCK) + N_EXPERTS) * MOE_BLOCK
    nb = n_slots // MOE_BLOCK
    nbp = -(-nb // LANE) * LANE

    w_main, w_f = _prep_w_in(w_in, tk=min(256, d))
    g_mix = norm_mix.reshape(depth, 1, d)
    g_mem = norm_mem.reshape(depth, 1, d)
    g_ffn = norm_ffn.reshape(depth, 1, d)
    g_final = norm_final.reshape(1, d)
    bias_f = _pad_lanes(b_forget.astype(f32)).reshape(depth, 1, LANE)
    w_kv = w_mem_kv.astype(bf16)
    zeros_f = jnp.zeros((depth, d, LANE), bf16)
    tables = _rel_bias_tables(rel_bias)
    wa, wb, wm, wo = (w.astype(bf16) for w in (w_br_a, w_br_b, w_br_m, w_out))
    wr = _pad_lanes(w_router).astype(bf16)
    br = _pad_lanes(b_router.reshape(1, -1)).astype(f32)
    w1b, w3b, w2b = (w.astype(bf16) for w in (w1, w3, w2))

    h = x.reshape(t, d)
    memf = mem.reshape(batch * n_mem, d)
    xb = jnp.zeros((n_slots, d // 2), jnp.uint32)

    for l in range(depth):
        proj, flog = _norm_proj(h, g_mix, w_main, w_f, l, n_sig_cols=3 * d, tm=tm_proj, tn=tn_proj)
        qaug, kaug = _forget_cumsum(flog, bias_f, l, batch=batch, seq=seq, tb=tq_b)
        kv, _ = _norm_proj(memf, g_mem, w_kv, zeros_f, l, n_sig_cols=0,
                           tm=min(1024, batch * n_mem), tn=W_M)

        oa = _attn_a(proj, tables, l, batch=batch, seq=seq, ng=ng)
        ob = _attn_b(proj, qaug, kaug, batch=batch, seq=seq, ng=ng, tq=tq_b)
        om = _attn_m(proj, kv, batch=batch, seq=seq, n_mem=n_mem, ng=ng, tq=512)

        h, xp, lt = _merge(h, proj, oa, ob, om, wa, wb, wm, wo, g_ffn, wr, br, l, tm=256)

        idx, wt, block_e, n_used = _route(lt, tn=512, nbp=nbp)
        dest = idx[2:4].reshape(2 * t)
        xb = _dispatch(dest, xp, xb, tc=tc)
        yb = _ffn(block_e[0, :nb], n_used[0, :1], xb, w1b, w3b, w2b, l)
        h = _combine(dest, h, wt, g_final, yb, tc=tc, final=(l == depth - 1))

    return h.reshape(batch, seq, d)
```

```python
import functools
import math

import jax
import jax.numpy as jnp
from jax import lax
from jax.experimental import pallas as pl
from jax.experimental.pallas import tpu as pltpu

CHUNK = 64
HEAD_DIM = 128
N_HEADS_A = 6
N_HEADS_B = 6
N_HEADS_M = 4
BAND_CHUNKS = 8
REL_CLIP = 128
N_EXPERTS = 16
N_GROUPS = 4
E_PER_GROUP = 4
MOE_BLOCK = 256
EPS = 1e-6
SCALE = HEAD_DIM ** -0.5
LOG2E = 1.4426950408889634

LANE = 128
SUBLANES = 8
BF16_ROWS = 16
NEG = -1e30
QA_BLOCK = 4 * CHUNK
BAND_BLOCKS = BAND_CHUNKS * CHUNK // QA_BLOCK + 1
N_QKV_BLOCKS = 3 * N_HEADS_A + 3 * N_HEADS_B
W_A = N_HEADS_A * HEAD_DIM
W_B = N_HEADS_B * HEAD_DIM
W_M = N_HEADS_M * HEAD_DIM
OFF_FB = 3 * W_A + 3 * W_B
OFF_QM = OFF_FB + N_HEADS_B
OFF_G = OFF_QM + W_M

f32 = jnp.float32
bf16 = jnp.bfloat16


def _cparams(sem, vmem_mb):
    return pltpu.CompilerParams(dimension_semantics=sem, vmem_limit_bytes=vmem_mb * 1024 * 1024)


def _layer_spec(shape, layer):
    nd = len(shape) - 1
    return pl.BlockSpec((None,) + tuple(shape[1:]), lambda *_: (layer,) + (0,) * nd,
                        pipeline_mode=pl.Buffered(1))


def _pack_halves(x):
    n = x.shape[1] // 2
    xf = x.astype(bf16).astype(f32)
    lo = pltpu.bitcast(xf[:, :n], jnp.uint32)
    hi = pltpu.bitcast(xf[:, n:], jnp.uint32)
    return lax.shift_right_logical(lo, jnp.uint32(16)) | (hi & jnp.uint32(0xFFFF0000))


def _unpack_halves(u):
    lo = pltpu.bitcast(lax.shift_left(u, jnp.uint32(16)), f32)
    hi = pltpu.bitcast(u & jnp.uint32(0xFFFF0000), f32)
    return lo, hi


def _const_spec(shape):
    nd = len(shape)
    return pl.BlockSpec(shape, lambda *_: (0,) * nd, pipeline_mode=pl.Buffered(1))


def _prep_body(w_ref, o_ref, f_ref, *, d):
    o_ref[:, :3 * d] = w_ref[:, OFF_G:OFF_G + 3 * d].astype(bf16)
    o_ref[:, 3 * d:3 * d + OFF_FB] = w_ref[:, :OFF_FB].astype(bf16)
    o_ref[:, 3 * d + OFF_FB:] = w_ref[:, OFF_QM:OFF_G].astype(bf16)
    lane = lax.broadcasted_iota(jnp.int32, (1, LANE), 1)
    f_ref[...] = jnp.where(lane < N_HEADS_B, w_ref[:, OFF_FB:OFF_FB + LANE], 0.0).astype(bf16)


def _prep_w_in(w_in, *, tk):
    depth, d, n_in = w_in.shape
    nm = 3 * d + OFF_FB + W_M
    return pl.pallas_call(
        functools.partial(_prep_body, d=d),
        grid=(depth, d // tk),
        in_specs=[pl.BlockSpec((None, tk, n_in), lambda l, k: (l, k, 0))],
        out_specs=[
            pl.BlockSpec((None, tk, nm), lambda l, k: (l, k, 0)),
            pl.BlockSpec((None, tk, LANE), lambda l, k: (l, k, 0)),
        ],
        out_shape=[
            jax.ShapeDtypeStruct((depth, d, nm), bf16),
            jax.ShapeDtypeStruct((depth, d, LANE), bf16),
        ],
        compiler_params=_cparams(("parallel", "parallel"), 48),
        name="prep_w_in",
    )(w_in)


def _proj_body(x_ref, g_ref, w_ref, wf_ref, o_ref, f_ref, xn_ref, *, n_sig, ncb):
    j = pl.program_id(1)

    @pl.when(j == 0)
    def _():
        x = x_ref[...]
        ms = jnp.mean(x * x, axis=-1, keepdims=True)
        xn = ((x * lax.rsqrt(ms + EPS)) * g_ref[...]).astype(bf16)
        xn_ref[...] = xn
        f_ref[...] = jnp.dot(xn, wf_ref[...], preferred_element_type=f32)

    acc = jnp.dot(xn_ref[...], w_ref[...], preferred_element_type=f32)
    for c in range(ncb):
        a = acc[:, c * LANE:(c + 1) * LANE]
        if n_sig > 0:
            a = jnp.where(j < n_sig, 1.0 / (1.0 + jnp.exp(-a)), a)
        o_ref[c] = a.astype(bf16)


def _norm_proj(x, g, w, wf, layer, *, n_sig_cols, tm, tn):
    m, d = x.shape
    n = w.shape[2]
    ncb = tn // LANE
    return pl.pallas_call(
        functools.partial(_proj_body, n_sig=n_sig_cols // tn, ncb=ncb),
        grid=(m // tm, n // tn),
        in_specs=[
            pl.BlockSpec((tm, d), lambda i, j: (i, 0)),
            pl.BlockSpec((None, 1, d), lambda i, j: (layer, 0, 0)),
            pl.BlockSpec((None, d, tn), lambda i, j: (layer, 0, j)),
            pl.BlockSpec((None, d, LANE), lambda i, j: (layer, 0, 0)),
        ],
        out_specs=[
            pl.BlockSpec((ncb, tm, LANE), lambda i, j: (j, i, 0)),
            pl.BlockSpec((tm, LANE), lambda i, j: (i, 0)),
        ],
        out_shape=[
            jax.ShapeDtypeStruct((n // LANE, m, LANE), bf16),
            jax.ShapeDtypeStruct((m, LANE), f32),
        ],
        scratch_shapes=[pltpu.VMEM((tm, d), bf16)],
        compiler_params=_cparams(("parallel", "arbitrary"), 56),
        name="norm_proj",
    )(x, g, w, wf)


def _split3(x):
    hi = x.astype(bf16)
    r1 = x - hi.astype(f32)
    mid = r1.astype(bf16)
    lo = (r1 - mid.astype(f32)).astype(bf16)
    return hi, mid, lo


def _fcum_body(fl_ref, b_ref, qa_ref, ka_ref, carry_ref, *, tb):
    @pl.when(pl.program_id(1) == 0)
    def _():
        carry_ref[...] = jnp.zeros_like(carry_ref)

    x = fl_ref[...] + b_ref[...]
    lf = jnp.minimum(x, 0.0) - jnp.log1p(jnp.exp(-jnp.abs(x)))
    hi, mid, lo = _split3(lf)
    row = lax.broadcasted_iota(jnp.int32, (tb, tb), 0)
    col = lax.broadcasted_iota(jnp.int32, (tb, tb), 1)
    tri = (col <= row).astype(bf16)
    c = (jnp.dot(tri, hi, preferred_element_type=f32)
         + jnp.dot(tri, mid, preferred_element_type=f32)
         + jnp.dot(tri, lo, preferred_element_type=f32))
    c = c + carry_ref[...]
    carry_ref[...] = c[tb - 1:tb, :]

    lane = lax.broadcasted_iota(jnp.int32, (1, LANE), 1)
    for h in range(N_HEADS_B):
        fh, fm, fl = (p.astype(f32) for p in _split3(c[:, h:h + 1] * LOG2E))
        qa = jnp.where(lane == 0, fh, jnp.where(lane == 1, fm, jnp.where(lane == 2, fl,
                       jnp.where(lane < 6, 1.0, 0.0))))
        ka = jnp.where(lane < 3, 1.0, jnp.where(lane == 3, -fh, jnp.where(lane == 4, -fm,
                       jnp.where(lane == 5, -fl, 0.0))))
        qa_ref[h] = qa.astype(bf16)
        ka_ref[h] = ka.astype(bf16)


def _forget_cumsum(flog, bias, layer, *, batch, seq, tb):
    nsb = seq // tb
    t = batch * seq
    return pl.pallas_call(
        functools.partial(_fcum_body, tb=tb),
        grid=(batch, nsb),
        in_specs=[
            pl.BlockSpec((tb, LANE), lambda b, s: (b * nsb + s, 0)),
            pl.BlockSpec((None, 1, LANE), lambda b, s: (layer, 0, 0)),
        ],
        out_specs=[
            pl.BlockSpec((N_HEADS_B, tb, LANE), lambda b, s: (0, b * nsb + s, 0)),
            pl.BlockSpec((N_HEADS_B, tb, LANE), lambda b, s: (0, b * nsb + s, 0)),
        ],
        out_shape=[
            jax.ShapeDtypeStruct((N_HEADS_B, t, LANE), bf16),
            jax.ShapeDtypeStruct((N_HEADS_B, t, LANE), bf16),
        ],
        scratch_shapes=[pltpu.VMEM((1, LANE), f32)],
        compiler_params=_cparams(("parallel", "arbitrary"), 32),
        name="forget_cumsum",
    )(flog, bias)


def _attn_a_body(q_ref, k_ref, v_ref, tb_ref, o_ref):
    i = pl.program_id(1)
    first = jnp.maximum(i - (BAND_BLOCKS - 1), 0)
    cb0 = jnp.maximum((BAND_BLOCKS - 1) - i, 0)
    for h in range(N_HEADS_A):
        q = (q_ref[h].astype(f32) * (SCALE * LOG2E)).astype(bf16)
        s_blocks = []
        for c in range(BAND_BLOCKS):
            ks = pl.multiple_of((first + c) * QA_BLOCK, QA_BLOCK)
            kb = k_ref[h, pl.ds(ks, QA_BLOCK), :]
            s = lax.dot_general(q, kb, (((1,), (1,)), ((), ())), preferred_element_type=f32)
            s_blocks.append(s + tb_ref[h, cb0 + c])
        m = s_blocks[0].max(axis=-1, keepdims=True)
        for s in s_blocks[1:]:
            m = jnp.maximum(m, s.max(axis=-1, keepdims=True))
        l = jnp.zeros_like(m)
        acc = jnp.zeros((QA_BLOCK, HEAD_DIM), f32)
        for c in range(BAND_BLOCKS):
            p = jnp.exp2(s_blocks[c] - m)
            l = l + p.sum(axis=-1, keepdims=True)
            ks = pl.multiple_of((first + c) * QA_BLOCK, QA_BLOCK)
            vb = v_ref[h, pl.ds(ks, QA_BLOCK), :]
            acc = acc + jnp.dot(p.astype(bf16), vb, preferred_element_type=f32)
        o_ref[h] = (acc / l).astype(bf16)


def _attn_a(proj, tables, layer, *, batch, seq, ng):
    nq = seq // QA_BLOCK
    base = ng // N_HEADS_A
    t = batch * seq
    return pl.pallas_call(
        _attn_a_body,
        grid=(batch, nq),
        in_specs=[
            pl.BlockSpec((N_HEADS_A, QA_BLOCK, LANE), lambda b, i: (base, b * nq + i, 0)),
            pl.BlockSpec((N_HEADS_A, seq, LANE), lambda b, i: (base + 1, b, 0)),
            pl.BlockSpec((N_HEADS_A, seq, LANE), lambda b, i: (base + 2, b, 0)),
            _layer_spec(tables.shape, layer),
        ],
        out_specs=pl.BlockSpec((N_HEADS_A, QA_BLOCK, LANE), lambda b, i: (0, b * nq + i, 0)),
        out_shape=jax.ShapeDtypeStruct((N_HEADS_A, t, LANE), bf16),
        compiler_params=_cparams(("parallel", "arbitrary"), 48),
        name="attn_chunk",
    )(proj, proj, proj, tables)


def _attn_b_body(q_ref, k_ref, v_ref, qa_ref, ka_ref, o_ref, *, tq):
    i = pl.program_id(1)
    row = lax.broadcasted_iota(jnp.int32, (tq, tq), 0)
    col = lax.broadcasted_iota(jnp.int32, (tq, tq), 1)
    causal = col <= row

    def one_head(h, qp, kb, carry, masked):
        m, l, acc = carry
        ks = pl.multiple_of(kb * tq, tq)
        kp = jnp.concatenate([k_ref[h, pl.ds(ks, tq), :], ka_ref[h, pl.ds(ks, tq), :]], axis=-1)
        s = lax.dot_general(qp, kp, (((1,), (1,)), ((), ())), preferred_element_type=f32)
        if masked:
            s = jnp.where(causal, s, NEG)
        m_new = jnp.maximum(m, s.max(axis=-1, keepdims=True))
        alpha = jnp.exp2(m - m_new)
        p = jnp.exp2(s - m_new)
        l = alpha * l + p.sum(axis=-1, keepdims=True)
        acc = alpha * acc + jnp.dot(p.astype(bf16), v_ref[h, pl.ds(ks, tq), :],
                                    preferred_element_type=f32)
        return m_new, l, acc

    for h0 in range(0, N_HEADS_B, 2):
        hs = (h0, h0 + 1)
        qps = [jnp.concatenate([(q_ref[h].astype(f32) * (SCALE * LOG2E)).astype(bf16), qa_ref[h]], axis=-1)
               for h in hs]

        def step(kb, carry, masked):
            return tuple(one_head(h, qp, kb, c, masked) for h, qp, c in zip(hs, qps, carry))

        init = (jnp.full((tq, 1), NEG, f32), jnp.zeros((tq, 1), f32), jnp.zeros((tq, HEAD_DIM), f32))
        carry = lax.fori_loop(0, i, lambda kb, c: step(kb, c, False), (init, init))
        for h, (m, l, acc) in zip(hs, step(i, carry, True)):
            o_ref[h] = (acc / l).astype(bf16)


def _attn_b(proj, qaug, kaug, *, batch, seq, ng, tq):
    nq = seq // tq
    base = ng // N_HEADS_B + 3
    t = batch * seq
    return pl.pallas_call(
        functools.partial(_attn_b_body, tq=tq),
        grid=(batch, nq),
        in_specs=[
            pl.BlockSpec((N_HEADS_B, tq, LANE), lambda b, i: (base, b * nq + i, 0)),
            pl.BlockSpec((N_HEADS_B, seq, LANE), lambda b, i: (base + 1, b, 0)),
            pl.BlockSpec((N_HEADS_B, seq, LANE), lambda b, i: (base + 2, b, 0)),
            pl.BlockSpec((N_HEADS_B, tq, LANE), lambda b, i: (0, b * nq + i, 0)),
            pl.BlockSpec((N_HEADS_B, seq, LANE), lambda b, i: (0, b, 0)),
        ],
        out_specs=pl.BlockSpec((N_HEADS_B, tq, LANE), lambda b, i: (0, b * nq + i, 0)),
        out_shape=jax.ShapeDtypeStruct((N_HEADS_B, t, LANE), bf16),
        compiler_params=_cparams(("parallel", "arbitrary"), 56),
        name="attn_forget",
    )(proj, proj, proj, qaug, kaug)


def _attn_m_body(q_ref, k_ref, v_ref, o_ref):
    for h in range(N_HEADS_M):
        s = lax.dot_general(q_ref[h], k_ref[h], (((1,), (1,)), ((), ())), preferred_element_type=f32)
        s = s * SCALE
        m = s.max(axis=-1, keepdims=True)
        p = jnp.exp(s - m)
        l = p.sum(axis=-1, keepdims=True)
        acc = jnp.dot(p.astype(bf16), v_ref[h], preferred_element_type=f32)
        o_ref[h] = (acc / l).astype(bf16)


def _attn_m(proj, kv, *, batch, seq, n_mem, ng, tq):
    nq = seq // tq
    base = (ng + N_QKV_BLOCKS) // N_HEADS_M
    t = batch * seq
    return pl.pallas_call(
        _attn_m_body,
        grid=(batch, nq),
        in_specs=[
            pl.BlockSpec((N_HEADS_M, tq, LANE), lambda b, i: (base, b * nq + i, 0)),
            pl.BlockSpec((N_HEADS_M, n_mem, LANE), lambda b, i: (0, b, 0)),
            pl.BlockSpec((N_HEADS_M, n_mem, LANE), lambda b, i: (1, b, 0)),
        ],
        out_specs=pl.BlockSpec((N_HEADS_M, tq, LANE), lambda b, i: (0, b * nq + i, 0)),
        out_shape=jax.ShapeDtypeStruct((N_HEADS_M, t, LANE), bf16),
        compiler_params=_cparams(("parallel", "arbitrary"), 32),
        name="attn_mem",
    )(proj, kv, kv)


def _merge_body(h_ref, g_ref, oa_ref, ob_ref, om_ref, wa_ref, wb_ref, wm_ref, wo_ref,
                gn_ref, wr_ref, br_ref, hn_ref, xp_ref, lt_ref, *, d):
    ndb = d // LANE

    def heads(ref, n):
        return jnp.concatenate([ref[h] for h in range(n)], axis=-1)

    def gate(br):
        return jnp.concatenate([g_ref[br * ndb + c] for c in range(ndb)], axis=-1).astype(f32)

    o_a = jnp.dot(heads(oa_ref, N_HEADS_A), wa_ref[...], preferred_element_type=f32)
    merged = gate(0) * o_a
    o_b = jnp.dot(heads(ob_ref, N_HEADS_B), wb_ref[...], preferred_element_type=f32)
    merged = merged + gate(1) * o_b
    o_m = jnp.dot(heads(om_ref, N_HEADS_M), wm_ref[...], preferred_element_type=f32)
    merged = merged + gate(2) * o_m
    hn = h_ref[...] + jnp.dot(merged.astype(bf16), wo_ref[...], preferred_element_type=f32)
    hn_ref[...] = hn

    ms = jnp.mean(hn * hn, axis=-1, keepdims=True)
    xn = ((hn * lax.rsqrt(ms + EPS)) * gn_ref[...]).astype(bf16)
    logits = jnp.dot(xn, wr_ref[...], preferred_element_type=f32) + br_ref[...]
    lt_ref[...] = logits.T[:N_EXPERTS, :]
    xp_ref[...] = _pack_halves(xn)


def _merge(h, proj, oa, ob, om, wa, wb, wm, wo, gn, wr, br, layer, *, tm):
    t, d = h.shape
    ng = 3 * d // LANE
    return pl.pallas_call(
        functools.partial(_merge_body, d=d),
        grid=(t // tm,),
        in_specs=[
            pl.BlockSpec((tm, d), lambda i: (i, 0)),
            pl.BlockSpec((ng, tm, LANE), lambda i: (0, i, 0)),
            pl.BlockSpec((N_HEADS_A, tm, LANE), lambda i: (0, i, 0)),
            pl.BlockSpec((N_HEADS_B, tm, LANE), lambda i: (0, i, 0)),
            pl.BlockSpec((N_HEADS_M, tm, LANE), lambda i: (0, i, 0)),
            _layer_spec(wa.shape, layer), _layer_spec(wb.shape, layer), _layer_spec(wm.shape, layer),
            _layer_spec(wo.shape, layer), _layer_spec(gn.shape, layer),
            _const_spec(wr.shape), _const_spec(br.shape),
        ],
        out_specs=[
            pl.BlockSpec((tm, d), lambda i: (i, 0)),
            pl.BlockSpec((tm, d // 2), lambda i: (i, 0)),
            pl.BlockSpec((N_EXPERTS, tm), lambda i: (0, i)),
        ],
        out_shape=[
            jax.ShapeDtypeStruct((t, d), f32),
            jax.ShapeDtypeStruct((t, d // 2), jnp.uint32),
            jax.ShapeDtypeStruct((N_EXPERTS, t), f32),
        ],
        compiler_params=_cparams(("parallel",), 56),
        name="merge_out_router",
    )(h, proj, oa, ob, om, wa, wb, wm, wo, gn, wr, br)


def _route_body(lt_ref, idx_ref, wt_ref, be_ref, nu_ref, run_ref, tot_ref, *, tn, nbp):
    phase = pl.program_id(0)
    i = pl.program_id(1)
    last = pl.num_programs(1) - 1

    @pl.when(i == 0)
    def _():
        run_ref[...] = jnp.zeros_like(run_ref)

    x = lt_ref[...]
    ex = jnp.exp(x - x.max(axis=0, keepdims=True))
    sc = ex / ex.sum(axis=0, keepdims=True)

    def top2(rows):
        m1 = functools.reduce(jnp.maximum, rows)
        i1 = jnp.full_like(m1, float(len(rows) - 1))
        for j in range(len(rows) - 2, -1, -1):
            i1 = jnp.where(rows[j] == m1, float(j), i1)
        rest = [jnp.where(i1 == float(j), -1.0, r) for j, r in enumerate(rows)]
        m2 = functools.reduce(jnp.maximum, rest)
        i2 = jnp.full_like(m2, float(len(rows) - 1))
        for j in range(len(rows) - 2, -1, -1):
            i2 = jnp.where(rest[j] == m2, float(j), i2)
        return m1, i1, m2, i2

    groups = [top2([sc[g * E_PER_GROUP + j:g * E_PER_GROUP + j + 1, :] for j in range(E_PER_GROUP)])
              for g in range(N_GROUPS)]
    gs = [g[0] + g[2] for g in groups]
    best = functools.reduce(jnp.maximum, gs)
    sel = [groups[N_GROUPS - 1][k] for k in range(4)]
    gi = jnp.full_like(best, float(N_GROUPS - 1))
    for g in range(N_GROUPS - 2, -1, -1):
        hit = gs[g] == best
        sel = [jnp.where(hit, groups[g][k], sel[k]) for k in range(4)]
        gi = jnp.where(hit, float(g), gi)
    m1, i1, m2, i2 = sel
    e0 = gi * E_PER_GROUP + i1
    e1 = gi * E_PER_GROUP + i2
    wsum = m1 + m2
    w0 = m1 / wsum
    w1 = m2 / wsum

    erow = lax.broadcasted_iota(jnp.int32, (N_EXPERTS, 1), 0).astype(f32)
    oh0 = (erow == e0).astype(f32)
    oh1 = (erow == e1).astype(f32)
    sel_mask = oh0 + oh1
    r = lax.broadcasted_iota(jnp.int32, (tn, tn), 0)
    c = lax.broadcasted_iota(jnp.int32, (tn, tn), 1)
    before = (r < c).astype(bf16)
    cnt = jnp.dot(sel_mask.astype(bf16), before, preferred_element_type=f32) + run_ref[:, 0:1]
    run_ref[...] = run_ref[...] + sel_mask.sum(axis=1, keepdims=True)

    @pl.when(jnp.logical_and(phase == 0, i == last))
    def _():
        tot_ref[...] = run_ref[...]

    @pl.when(phase == 1)
    def _():
        tot = tot_ref[:, 0:1]
        padded = jnp.floor((tot + (MOE_BLOCK - 1)) / MOE_BLOCK) * MOE_BLOCK
        start = jnp.zeros_like(padded)
        for e in range(N_EXPERTS - 1):
            start = start + jnp.where(erow > float(e), padded[e:e + 1, :], 0.0)
        slot = start + cnt
        d0 = (oh0 * slot).sum(axis=0, keepdims=True)
        d1 = (oh1 * slot).sum(axis=0, keepdims=True)
        r8 = lax.broadcasted_iota(jnp.int32, (8, 1), 0)
        rows = jnp.where(r8 == 0, e0, jnp.where(r8 == 1, e1, jnp.where(r8 == 2, d0, jnp.where(r8 == 3, d1, 0.0))))
        idx_ref[...] = rows.astype(jnp.int32)
        rl = lax.broadcasted_iota(jnp.int32, (LANE, 1), 0)
        wt_ref[...] = jnp.where(rl == 0, w0, jnp.where(rl == 1, w1, 0.0)).T
        end = start + padded
        blk = lax.broadcasted_iota(jnp.int32, (1, nbp), 1).astype(f32) * MOE_BLOCK
        be = (end <= blk).astype(f32).sum(axis=0, keepdims=True)
        be_ref[...] = jnp.minimum(be, N_EXPERTS - 1.0).astype(jnp.int32)
        nu_ref[...] = jnp.broadcast_to(end[N_EXPERTS - 1:, :] / MOE_BLOCK, (1, LANE)).astype(jnp.int32)


def _route(lt, *, tn, nbp):
    t = lt.shape[1]
    return pl.pallas_call(
        functools.partial(_route_body, tn=tn, nbp=nbp),
        grid=(2, t // tn),
        in_specs=[pl.BlockSpec((N_EXPERTS, tn), lambda p, i: (0, i))],
        out_specs=[
            pl.BlockSpec((8, tn), lambda p, i: (0, i * p)),
            pl.BlockSpec((tn, LANE), lambda p, i: (i * p, 0)),
            pl.BlockSpec((1, nbp), lambda p, i: (0, 0)),
            pl.BlockSpec((1, LANE), lambda p, i: (0, 0)),
        ],
        out_shape=[
            jax.ShapeDtypeStruct((8, t), jnp.int32),
            jax.ShapeDtypeStruct((t, LANE), f32),
            jax.ShapeDtypeStruct((1, nbp), jnp.int32),
            jax.ShapeDtypeStruct((1, LANE), jnp.int32),
        ],
        scratch_shapes=[pltpu.VMEM((N_EXPERTS, LANE), f32), pltpu.VMEM((N_EXPERTS, LANE), f32)],
        compiler_params=_cparams(("arbitrary", "arbitrary"), 32),
        name="route",
    )(lt)


def _dispatch_body(dest_ref, x_ref, prev_ref, xb_ref, sem, *, tc, t):
    del prev_ref
    base = pl.program_id(0) * tc

    def copy(g, u, d):
        return pltpu.make_async_copy(x_ref.at[g, pl.ds(u, 1), :], xb_ref.at[pl.ds(d, 1), :], sem)

    def issue(g, carry):
        for u in range(SUBLANES):
            copy(g, u, dest_ref[base + g * SUBLANES + u]).start()
            copy(g, u, dest_ref[t + base + g * SUBLANES + u]).start()
        return carry

    lax.fori_loop(0, tc // SUBLANES, issue, 0)

    def drain(g, carry):
        for _ in range(2 * SUBLANES):
            copy(0, 0, 0).wait()
        return carry

    lax.fori_loop(0, tc // SUBLANES, drain, 0)


def _dispatch(dest, xp, slots, *, tc):
    t, dh = xp.shape
    xp = xp.reshape(t // SUBLANES, SUBLANES, dh)
    grid_spec = pltpu.PrefetchScalarGridSpec(
        num_scalar_prefetch=1,
        grid=(t // tc,),
        in_specs=[
            pl.BlockSpec((tc // SUBLANES, SUBLANES, dh), lambda i, dest: (i, 0, 0)),
            pl.BlockSpec(memory_space=pl.ANY),
        ],
        out_specs=pl.BlockSpec(memory_space=pl.ANY),
        scratch_shapes=[pltpu.SemaphoreType.DMA],
    )
    return pl.pallas_call(
        functools.partial(_dispatch_body, tc=tc, t=t),
        grid_spec=grid_spec,
        out_shape=jax.ShapeDtypeStruct(slots.shape, slots.dtype),
        input_output_aliases={2: 0},
        compiler_params=_cparams(("arbitrary",), 32),
        name="dispatch",
    )(dest, xp, slots)


def _ffn_body(be_ref, nu_ref, x_ref, w1_ref, w3_ref, w2_ref, y_ref, *, dh):
    del be_ref
    b = pl.program_id(0)

    @pl.when(b < nu_ref[0])
    def _():
        lo, hi = (p.astype(bf16) for p in _unpack_halves(x_ref[...]))
        h1 = (jnp.dot(lo, w1_ref[:dh, :], preferred_element_type=f32)
              + jnp.dot(hi, w1_ref[dh:, :], preferred_element_type=f32))
        h3 = (jnp.dot(lo, w3_ref[:dh, :], preferred_element_type=f32)
              + jnp.dot(hi, w3_ref[dh:, :], preferred_element_type=f32))
        hid = (h1 / (1.0 + jnp.exp(-h1))) * h3
        y_ref[...] = _pack_halves(jnp.dot(hid.astype(bf16), w2_ref[...], preferred_element_type=f32))

    @pl.when(b >= nu_ref[0])
    def _():
        y_ref[...] = jnp.zeros_like(y_ref)


def _ffn(block_e, n_used, xb, w1, w3, w2, layer):
    ns, dh = xb.shape
    _, _, d, f = w1.shape
    nb = ns // MOE_BLOCK
    grid_spec = pltpu.PrefetchScalarGridSpec(
        num_scalar_prefetch=2,
        grid=(nb,),
        in_specs=[
            pl.BlockSpec((MOE_BLOCK, dh), lambda b, be, nu: (b, 0)),
            pl.BlockSpec((None, None, d, f), lambda b, be, nu: (layer, be[b], 0, 0)),
            pl.BlockSpec((None, None, d, f), lambda b, be, nu: (layer, be[b], 0, 0)),
            pl.BlockSpec((None, None, f, d), lambda b, be, nu: (layer, be[b], 0, 0)),
        ],
        out_specs=pl.BlockSpec((MOE_BLOCK, dh), lambda b, be, nu: (b, 0)),
    )
    return pl.pallas_call(
        functools.partial(_ffn_body, dh=dh),
        grid_spec=grid_spec,
        out_shape=jax.ShapeDtypeStruct((ns, dh), jnp.uint32),
        compiler_params=_cparams(("arbitrary",), 56),
        name="expert_ffn",
    )(block_e, n_used, xb, w1, w3, w2)


def _combine_body(dest_ref, h_ref, wt_ref, g_ref, y_ref, o_ref, buf0, buf1, sem, *, tc, t, final):
    base = pl.program_id(0) * tc

    def copy(d, buf, g, u):
        return pltpu.make_async_copy(y_ref.at[pl.ds(d, 1), :], buf.at[g, pl.ds(u, 1), :], sem)

    def issue(g, carry):
        for u in range(SUBLANES):
            copy(dest_ref[base + g * SUBLANES + u], buf0, g, u).start()
            copy(dest_ref[t + base + g * SUBLANES + u], buf1, g, u).start()
        return carry

    lax.fori_loop(0, tc // SUBLANES, issue, 0)

    def drain(g, carry):
        for _ in range(SUBLANES):
            copy(0, buf0, 0, 0).wait()
            copy(0, buf1, 0, 0).wait()
        return carry

    lax.fori_loop(0, tc // SUBLANES, drain, 0)
    w = wt_ref[...]
    dh = buf0.shape[2]
    lo0, hi0 = _unpack_halves(buf0[...])
    lo1, hi1 = _unpack_halves(buf1[...])
    out_lo = h_ref[:, :, :dh] + (w[:, :, 0:1] * lo0 + w[:, :, 1:2] * lo1)
    out_hi = h_ref[:, :, dh:] + (w[:, :, 0:1] * hi0 + w[:, :, 1:2] * hi1)
    if final:
        ss = (out_lo * out_lo).sum(axis=-1, keepdims=True) + (out_hi * out_hi).sum(axis=-1, keepdims=True)
        inv = lax.rsqrt(ss / (2 * dh) + EPS)
        out_lo = (out_lo * inv) * g_ref[:, :dh]
        out_hi = (out_hi * inv) * g_ref[:, dh:]
    o_ref[:, :, :dh] = out_lo
    o_ref[:, :, dh:] = out_hi


def _combine(dest, h, wt, g, y, *, tc, final):
    t, d = h.shape
    tg = tc // SUBLANES
    view = lambda a: a.reshape(t // SUBLANES, SUBLANES, a.shape[1])
    grid_spec = pltpu.PrefetchScalarGridSpec(
        num_scalar_prefetch=1,
        grid=(t // tc,),
        in_specs=[
            pl.BlockSpec((tg, SUBLANES, d), lambda i, dest: (i, 0, 0)),
            pl.BlockSpec((tg, SUBLANES, LANE), lambda i, dest: (i, 0, 0)),
            pl.BlockSpec((1, d), lambda i, dest: (0, 0)),
            pl.BlockSpec(memory_space=pl.ANY),
        ],
        out_specs=pl.BlockSpec((tg, SUBLANES, d), lambda i, dest: (i, 0, 0)),
        scratch_shapes=[pltpu.VMEM((tg, SUBLANES, d // 2), jnp.uint32),
                        pltpu.VMEM((tg, SUBLANES, d // 2), jnp.uint32),
                        pltpu.SemaphoreType.DMA],
    )
    out = pl.pallas_call(
        functools.partial(_combine_body, tc=tc, t=t, final=final),
        grid_spec=grid_spec,
        out_shape=jax.ShapeDtypeStruct((t // SUBLANES, SUBLANES, d), f32),
        compiler_params=_cparams(("arbitrary",), 48),
        name="combine",
    )(dest, view(h), view(wt), g, y)
    return out.reshape(t, d)


def _rel_bias_tables(rel_bias):
    depth, heads, _ = rel_bias.shape
    band0 = (BAND_BLOCKS - 1) * QA_BLOCK
    ncol = (2 * BAND_BLOCKS - 1) * QA_BLOCK
    n = ncol + QA_BLOCK
    rb = rel_bias.astype(f32)
    n_lo = band0 - REL_CLIP + QA_BLOCK - 1
    w = jnp.concatenate([
        jnp.broadcast_to(rb[..., :1], (depth, heads, n_lo)), rb,
        jnp.broadcast_to(rb[..., -1:], (depth, heads, n + 1 - n_lo - rb.shape[-1]))], axis=-1)
    skew = jnp.broadcast_to(w[:, :, None, :], (depth, heads, QA_BLOCK, n + 1))
    skew = skew.reshape(depth, heads, QA_BLOCK * (n + 1))[..., :QA_BLOCK * n]
    tab = skew.reshape(depth, heads, QA_BLOCK, n)[..., QA_BLOCK - 1:QA_BLOCK - 1 + ncol]
    row = jnp.arange(QA_BLOCK)[:, None]
    u = jnp.arange(ncol)[None, :]
    kc = u // CHUNK
    qc = row // CHUNK + BAND_CHUNKS
    valid = (kc >= qc - BAND_CHUNKS) & (kc <= qc)
    tab = jnp.where(valid, tab * LOG2E, NEG)
    return tab.reshape(depth, heads, QA_BLOCK, 2 * BAND_BLOCKS - 1, QA_BLOCK).transpose(0, 1, 3, 2, 4)


def _pad_lanes(w):
    pad = [(0, 0)] * (w.ndim - 1) + [(0, LANE - w.shape[-1])]
    return jnp.pad(w, pad)


def kernel(x, mem, norm_mix, w_in, b_forget, rel_bias, norm_mem, w_mem_kv, w_br_a, w_br_b, w_br_m,
           w_out, norm_ffn, w_router, b_router, w1, w3, w2, norm_final):
    batch, seq, d = x.shape
    n_mem = mem.shape[1]
    depth = w_in.shape[0]
    t = batch * seq
    ng = 3 * d // LANE
    assert ng % N_HEADS_A == 0 and (ng + N_QKV_BLOCKS) % N_HEADS_M == 0
    assert seq % 512 == 0 and n_mem % LANE == 0

    tn_proj = min(1024, math.gcd(3 * d, OFF_FB + W_M))
    tm_proj = min(1024, t)
    tq_b = 512
    tc = min(512, t)
    n_slots = (-(-(t * 2) // MOE_BLOCK) + N_EXPERTS) * MOE_BLOCK
    nb = n_slots // MOE_BLOCK
    nbp = -(-nb // LANE) * LANE

    w_main, w_f = _prep_w_in(w_in.astype(bf16), tk=min(256, d))
    g_mix = norm_mix.reshape(depth, 1, d)
    g_mem = norm_mem.reshape(depth, 1, d)
    g_ffn = norm_ffn.reshape(depth, 1, d)
    g_final = norm_final.reshape(1, d)
    bias_f = _pad_lanes(b_forget.astype(f32)).reshape(depth, 1, LANE)
    w_kv = w_mem_kv.astype(bf16)
    zeros_f = jnp.zeros((depth, d, LANE), bf16)
    tables = _rel_bias_tables(rel_bias)
    wa, wb, wm, wo = (w.astype(bf16) for w in (w_br_a, w_br_b, w_br_m, w_out))
    wr = _pad_lanes(w_router).astype(bf16)
    br = _pad_lanes(b_router.reshape(1, -1)).astype(f32)
    w1b, w3b, w2b = (w.astype(bf16) for w in (w1, w3, w2))

    h = x.reshape(t, d)
    memf = mem.reshape(batch * n_mem, d)
    xb = jnp.zeros((n_slots, d // 2), jnp.uint32)

    for l in range(depth):
        proj, flog = _norm_proj(h, g_mix, w_main, w_f, l, n_sig_cols=3 * d, tm=tm_proj, tn=tn_proj)
        qaug, kaug = _forget_cumsum(flog, bias_f, l, batch=batch, seq=seq, tb=tq_b)
        kv, _ = _norm_proj(memf, g_mem, w_kv, zeros_f, l, n_sig_cols=0,
                           tm=min(1024, batch * n_mem), tn=W_M)

        oa = _attn_a(proj, tables, l, batch=batch, seq=seq, ng=ng)
        ob = _attn_b(proj, qaug, kaug, batch=batch, seq=seq, ng=ng, tq=tq_b)
        om = _attn_m(proj, kv, batch=batch, seq=seq, n_mem=n_mem, ng=ng, tq=512)

        h, xp, lt = _merge(h, proj, oa, ob, om, wa, wb, wm, wo, g_ffn, wr, br, l, tm=256)

        idx, wt, block_e, n_used = _route(lt, tn=512, nbp=nbp)
        dest = idx[2:4].reshape(2 * t)
        xb = _dispatch(dest, xp, xb, tc=tc)
        yb = _ffn(block_e[0, :nb], n_used[0, :1], xb, w1b, w3b, w2b, l)
        h = _combine(dest, h, wt, g_final, yb, tc=tc, final=(l == depth - 1))

    return h.reshape(batch, seq, d)
```

```python
import functools
import math

import jax
import jax.numpy as jnp
from jax import lax
from jax.experimental import pallas as pl
from jax.experimental.pallas import tpu as pltpu

CHUNK = 64
HEAD_DIM = 128
N_HEADS_A = 6
N_HEADS_B = 6
N_HEADS_M = 4
BAND_CHUNKS = 8
REL_CLIP = 128
N_EXPERTS = 16
N_GROUPS = 4
E_PER_GROUP = 4
MOE_BLOCK = 256
EPS = 1e-6
SCALE = HEAD_DIM ** -0.5
LOG2E = 1.4426950408889634

LANE = 128
SUBLANES = 8
BF16_ROWS = 16
NEG = -1e30
QA_BLOCK = 4 * CHUNK
BAND_BLOCKS = BAND_CHUNKS * CHUNK // QA_BLOCK + 1
N_QKV_BLOCKS = 3 * N_HEADS_A + 3 * N_HEADS_B
W_A = N_HEADS_A * HEAD_DIM
W_B = N_HEADS_B * HEAD_DIM
W_M = N_HEADS_M * HEAD_DIM
OFF_FB = 3 * W_A + 3 * W_B
OFF_QM = OFF_FB + N_HEADS_B
OFF_G = OFF_QM + W_M

f32 = jnp.float32
bf16 = jnp.bfloat16


def _cparams(sem, vmem_mb):
    return pltpu.CompilerParams(dimension_semantics=sem, vmem_limit_bytes=vmem_mb * 1024 * 1024)


def _layer_spec(shape, layer):
    nd = len(shape) - 1
    return pl.BlockSpec((None,) + tuple(shape[1:]), lambda *_: (layer,) + (0,) * nd,
                        pipeline_mode=pl.Buffered(1))


def _pack_halves(x):
    n = x.shape[1] // 2
    xf = x.astype(bf16).astype(f32)
    lo = pltpu.bitcast(xf[:, :n], jnp.uint32)
    hi = pltpu.bitcast(xf[:, n:], jnp.uint32)
    return lax.shift_right_logical(lo, jnp.uint32(16)) | (hi & jnp.uint32(0xFFFF0000))


def _unpack_halves(u):
    lo = pltpu.bitcast(lax.shift_left(u, jnp.uint32(16)), f32)
    hi = pltpu.bitcast(u & jnp.uint32(0xFFFF0000), f32)
    return lo, hi


def _store_row_tiles(ref, packed):
    m, w = packed.shape
    r = w // LANE
    for k in range(r):
        ref[pl.ds(k, m, stride=r), :] = packed[:, k * LANE:(k + 1) * LANE]


def _load_row_tiles(ref, m):
    r = ref.shape[0] // m
    return [ref[pl.ds(k, m, stride=r), :] for k in range(r)]


def _const_spec(shape):
    nd = len(shape)
    return pl.BlockSpec(shape, lambda *_: (0,) * nd, pipeline_mode=pl.Buffered(1))


def _prep_body(w_ref, o_ref, f_ref, *, d):
    o_ref[:, :3 * d] = w_ref[:, OFF_G:OFF_G + 3 * d].astype(bf16)
    o_ref[:, 3 * d:3 * d + OFF_FB] = w_ref[:, :OFF_FB].astype(bf16)
    o_ref[:, 3 * d + OFF_FB:] = w_ref[:, OFF_QM:OFF_G].astype(bf16)
    lane = lax.broadcasted_iota(jnp.int32, (1, LANE), 1)
    f_ref[...] = jnp.where(lane < N_HEADS_B, w_ref[:, OFF_FB:OFF_FB + LANE], 0.0).astype(bf16)


def _prep_w_in(w_in, *, tk):
    depth, d, n_in = w_in.shape
    nm = 3 * d + OFF_FB + W_M
    return pl.pallas_call(
        functools.partial(_prep_body, d=d),
        grid=(depth, d // tk),
        in_specs=[pl.BlockSpec((None, tk, n_in), lambda l, k: (l, k, 0))],
        out_specs=[
            pl.BlockSpec((None, tk, nm), lambda l, k: (l, k, 0)),
            pl.BlockSpec((None, tk, LANE), lambda l, k: (l, k, 0)),
        ],
        out_shape=[
            jax.ShapeDtypeStruct((depth, d, nm), bf16),
            jax.ShapeDtypeStruct((depth, d, LANE), bf16),
        ],
        compiler_params=_cparams(("parallel", "parallel"), 48),
        name="prep_w_in",
    )(w_in)


def _proj_body(x_ref, g_ref, w_ref, wf_ref, o_ref, f_ref, xn_ref, *, n_sig, ncb):
    j = pl.program_id(1)

    @pl.when(j == 0)
    def _():
        x = x_ref[...]
        ms = jnp.mean(x * x, axis=-1, keepdims=True)
        xn = ((x * lax.rsqrt(ms + EPS)) * g_ref[...]).astype(bf16)
        xn_ref[...] = xn
        f_ref[...] = jnp.dot(xn, wf_ref[...], preferred_element_type=f32)

    acc = jnp.dot(xn_ref[...], w_ref[...], preferred_element_type=f32)
    for c in range(ncb):
        a = acc[:, c * LANE:(c + 1) * LANE]
        if n_sig > 0:
            a = jnp.where(j < n_sig, 1.0 / (1.0 + jnp.exp(-a)), a)
        o_ref[c] = a.astype(bf16)


def _norm_proj(x, g, w, wf, layer, *, n_sig_cols, tm, tn):
    m, d = x.shape
    n = w.shape[2]
    ncb = tn // LANE
    return pl.pallas_call(
        functools.partial(_proj_body, n_sig=n_sig_cols // tn, ncb=ncb),
        grid=(m // tm, n // tn),
        in_specs=[
            pl.BlockSpec((tm, d), lambda i, j: (i, 0)),
            pl.BlockSpec((None, 1, d), lambda i, j: (layer, 0, 0)),
            pl.BlockSpec((None, d, tn), lambda i, j: (layer, 0, j)),
            pl.BlockSpec((None, d, LANE), lambda i, j: (layer, 0, 0)),
        ],
        out_specs=[
            pl.BlockSpec((ncb, tm, LANE), lambda i, j: (j, i, 0)),
            pl.BlockSpec((tm, LANE), lambda i, j: (i, 0)),
        ],
        out_shape=[
            jax.ShapeDtypeStruct((n // LANE, m, LANE), bf16),
            jax.ShapeDtypeStruct((m, LANE), f32),
        ],
        scratch_shapes=[pltpu.VMEM((tm, d), bf16)],
        compiler_params=_cparams(("parallel", "arbitrary"), 56),
        name="norm_proj",
    )(x, g, w, wf)


def _split3(x):
    hi = x.astype(bf16)
    r1 = x - hi.astype(f32)
    mid = r1.astype(bf16)
    lo = (r1 - mid.astype(f32)).astype(bf16)
    return hi, mid, lo


def _fcum_body(fl_ref, b_ref, qa_ref, ka_ref, carry_ref, *, tb):
    @pl.when(pl.program_id(1) == 0)
    def _():
        carry_ref[...] = jnp.zeros_like(carry_ref)

    x = fl_ref[...] + b_ref[...]
    lf = jnp.minimum(x, 0.0) - jnp.log1p(jnp.exp(-jnp.abs(x)))
    hi, mid, lo = _split3(lf)
    row = lax.broadcasted_iota(jnp.int32, (tb, tb), 0)
    col = lax.broadcasted_iota(jnp.int32, (tb, tb), 1)
    tri = (col <= row).astype(bf16)
    c = (jnp.dot(tri, hi, preferred_element_type=f32)
         + jnp.dot(tri, mid, preferred_element_type=f32)
         + jnp.dot(tri, lo, preferred_element_type=f32))
    c = c + carry_ref[...]
    carry_ref[...] = c[tb - 1:tb, :]

    lane = lax.broadcasted_iota(jnp.int32, (1, LANE), 1)
    for h in range(N_HEADS_B):
        fh, fm, fl = (p.astype(f32) for p in _split3(c[:, h:h + 1] * LOG2E))
        qa = jnp.where(lane == 0, fh, jnp.where(lane == 1, fm, jnp.where(lane == 2, fl,
                       jnp.where(lane < 6, 1.0, 0.0))))
        ka = jnp.where(lane < 3, 1.0, jnp.where(lane == 3, -fh, jnp.where(lane == 4, -fm,
                       jnp.where(lane == 5, -fl, 0.0))))
        qa_ref[h] = qa.astype(bf16)
        ka_ref[h] = ka.astype(bf16)


def _forget_cumsum(flog, bias, layer, *, batch, seq, tb):
    nsb = seq // tb
    t = batch * seq
    return pl.pallas_call(
        functools.partial(_fcum_body, tb=tb),
        grid=(batch, nsb),
        in_specs=[
            pl.BlockSpec((tb, LANE), lambda b, s: (b * nsb + s, 0)),
            pl.BlockSpec((None, 1, LANE), lambda b, s: (layer, 0, 0)),
        ],
        out_specs=[
            pl.BlockSpec((N_HEADS_B, tb, LANE), lambda b, s: (0, b * nsb + s, 0)),
            pl.BlockSpec((N_HEADS_B, tb, LANE), lambda b, s: (0, b * nsb + s, 0)),
        ],
        out_shape=[
            jax.ShapeDtypeStruct((N_HEADS_B, t, LANE), bf16),
            jax.ShapeDtypeStruct((N_HEADS_B, t, LANE), bf16),
        ],
        scratch_shapes=[pltpu.VMEM((1, LANE), f32)],
        compiler_params=_cparams(("parallel", "arbitrary"), 32),
        name="forget_cumsum",
    )(flog, bias)


def _attn_a_body(q_ref, k_ref, v_ref, tb_ref, o_ref):
    i = pl.program_id(1)
    first = jnp.maximum(i - (BAND_BLOCKS - 1), 0)
    cb0 = jnp.maximum((BAND_BLOCKS - 1) - i, 0)
    for h in range(N_HEADS_A):
        q = (q_ref[h].astype(f32) * (SCALE * LOG2E)).astype(bf16)
        s_blocks = []
        for c in range(BAND_BLOCKS):
            ks = pl.multiple_of((first + c) * QA_BLOCK, QA_BLOCK)
            kb = k_ref[h, pl.ds(ks, QA_BLOCK), :]
            s = lax.dot_general(q, kb, (((1,), (1,)), ((), ())), preferred_element_type=f32)
            s_blocks.append(s + tb_ref[h, cb0 + c])
        m = s_blocks[0].max(axis=-1, keepdims=True)
        for s in s_blocks[1:]:
            m = jnp.maximum(m, s.max(axis=-1, keepdims=True))
        l = jnp.zeros_like(m)
        acc = jnp.zeros((QA_BLOCK, HEAD_DIM), f32)
        for c in range(BAND_BLOCKS):
            p = jnp.exp2(s_blocks[c] - m)
            l = l + p.sum(axis=-1, keepdims=True)
            ks = pl.multiple_of((first + c) * QA_BLOCK, QA_BLOCK)
            vb = v_ref[h, pl.ds(ks, QA_BLOCK), :]
            acc = acc + jnp.dot(p.astype(bf16), vb, preferred_element_type=f32)
        o_ref[h] = (acc / l).astype(bf16)


def _attn_a(proj, tables, layer, *, batch, seq, ng):
    nq = seq // QA_BLOCK
    base = ng // N_HEADS_A
    t = batch * seq
    return pl.pallas_call(
        _attn_a_body,
        grid=(batch, nq),
        in_specs=[
            pl.BlockSpec((N_HEADS_A, QA_BLOCK, LANE), lambda b, i: (base, b * nq + i, 0)),
            pl.BlockSpec((N_HEADS_A, seq, LANE), lambda b, i: (base + 1, b, 0)),
            pl.BlockSpec((N_HEADS_A, seq, LANE), lambda b, i: (base + 2, b, 0)),
            _layer_spec(tables.shape, layer),
        ],
        out_specs=pl.BlockSpec((N_HEADS_A, QA_BLOCK, LANE), lambda b, i: (0, b * nq + i, 0)),
        out_shape=jax.ShapeDtypeStruct((N_HEADS_A, t, LANE), bf16),
        compiler_params=_cparams(("parallel", "arbitrary"), 48),
        name="attn_chunk",
    )(proj, proj, proj, tables)


def _attn_b_body(q_ref, k_ref, v_ref, qa_ref, ka_ref, o_ref, *, tq):
    i = pl.program_id(1)
    row = lax.broadcasted_iota(jnp.int32, (tq, tq), 0)
    col = lax.broadcasted_iota(jnp.int32, (tq, tq), 1)
    causal = col <= row

    def one_head(h, qp, kb, carry, masked):
        m, l, acc = carry
        ks = pl.multiple_of(kb * tq, tq)
        kp = jnp.concatenate([k_ref[h, pl.ds(ks, tq), :], ka_ref[h, pl.ds(ks, tq), :]], axis=-1)
        s = lax.dot_general(qp, kp, (((1,), (1,)), ((), ())), preferred_element_type=f32)
        if masked:
            s = jnp.where(causal, s, NEG)
        m_new = jnp.maximum(m, s.max(axis=-1, keepdims=True))
        alpha = jnp.exp2(m - m_new)
        p = jnp.exp2(s - m_new)
        l = alpha * l + p.sum(axis=-1, keepdims=True)
        acc = alpha * acc + jnp.dot(p.astype(bf16), v_ref[h, pl.ds(ks, tq), :],
                                    preferred_element_type=f32)
        return m_new, l, acc

    for h0 in range(0, N_HEADS_B, 3):
        hs = (h0, h0 + 1, h0 + 2)
        qps = [jnp.concatenate([(q_ref[h].astype(f32) * (SCALE * LOG2E)).astype(bf16), qa_ref[h]], axis=-1)
               for h in hs]

        def step(kb, carry, masked):
            return tuple(one_head(h, qp, kb, c, masked) for h, qp, c in zip(hs, qps, carry))

        init = (jnp.full((tq, 1), NEG, f32), jnp.zeros((tq, 1), f32), jnp.zeros((tq, HEAD_DIM), f32))
        carry = lax.fori_loop(0, i, lambda kb, c: step(kb, c, False), (init,) * len(hs))
        for h, (m, l, acc) in zip(hs, step(i, carry, True)):
            o_ref[h] = (acc / l).astype(bf16)


def _attn_b(proj, qaug, kaug, *, batch, seq, ng, tq):
    nq = seq // tq
    base = ng // N_HEADS_B + 3
    t = batch * seq
    return pl.pallas_call(
        functools.partial(_attn_b_body, tq=tq),
        grid=(batch, nq),
        in_specs=[
            pl.BlockSpec((N_HEADS_B, tq, LANE), lambda b, i: (base, b * nq + i, 0)),
            pl.BlockSpec((N_HEADS_B, seq, LANE), lambda b, i: (base + 1, b, 0)),
            pl.BlockSpec((N_HEADS_B, seq, LANE), lambda b, i: (base + 2, b, 0)),
            pl.BlockSpec((N_HEADS_B, tq, LANE), lambda b, i: (0, b * nq + i, 0)),
            pl.BlockSpec((N_HEADS_B, seq, LANE), lambda b, i: (0, b, 0)),
        ],
        out_specs=pl.BlockSpec((N_HEADS_B, tq, LANE), lambda b, i: (0, b * nq + i, 0)),
        out_shape=jax.ShapeDtypeStruct((N_HEADS_B, t, LANE), bf16),
        compiler_params=_cparams(("parallel", "arbitrary"), 56),
        name="attn_forget",
    )(proj, proj, proj, qaug, kaug)


def _attn_m_body(q_ref, k_ref, v_ref, o_ref):
    for h in range(N_HEADS_M):
        s = lax.dot_general(q_ref[h], k_ref[h], (((1,), (1,)), ((), ())), preferred_element_type=f32)
        s = s * SCALE
        m = s.max(axis=-1, keepdims=True)
        p = jnp.exp(s - m)
        l = p.sum(axis=-1, keepdims=True)
        acc = jnp.dot(p.astype(bf16), v_ref[h], preferred_element_type=f32)
        o_ref[h] = (acc / l).astype(bf16)


def _attn_m(proj, kv, *, batch, seq, n_mem, ng, tq):
    nq = seq // tq
    base = (ng + N_QKV_BLOCKS) // N_HEADS_M
    t = batch * seq
    return pl.pallas_call(
        _attn_m_body,
        grid=(batch, nq),
        in_specs=[
            pl.BlockSpec((N_HEADS_M, tq, LANE), lambda b, i: (base, b * nq + i, 0)),
            pl.BlockSpec((N_HEADS_M, n_mem, LANE), lambda b, i: (0, b, 0)),
            pl.BlockSpec((N_HEADS_M, n_mem, LANE), lambda b, i: (1, b, 0)),
        ],
        out_specs=pl.BlockSpec((N_HEADS_M, tq, LANE), lambda b, i: (0, b * nq + i, 0)),
        out_shape=jax.ShapeDtypeStruct((N_HEADS_M, t, LANE), bf16),
        compiler_params=_cparams(("parallel", "arbitrary"), 32),
        name="attn_mem",
    )(proj, kv, kv)


def _merge_body(h_ref, g_ref, oa_ref, ob_ref, om_ref, wa_ref, wb_ref, wm_ref, wo_ref,
                gn_ref, wr_ref, br_ref, hn_ref, xp_ref, lt_ref, *, d):
    ndb = d // LANE

    def heads(ref, n):
        return jnp.concatenate([ref[h] for h in range(n)], axis=-1)

    def gate(br):
        return jnp.concatenate([g_ref[br * ndb + c] for c in range(ndb)], axis=-1).astype(f32)

    o_a = jnp.dot(heads(oa_ref, N_HEADS_A), wa_ref[...], preferred_element_type=f32)
    merged = gate(0) * o_a
    o_b = jnp.dot(heads(ob_ref, N_HEADS_B), wb_ref[...], preferred_element_type=f32)
    merged = merged + gate(1) * o_b
    o_m = jnp.dot(heads(om_ref, N_HEADS_M), wm_ref[...], preferred_element_type=f32)
    merged = merged + gate(2) * o_m
    hn = h_ref[...] + jnp.dot(merged.astype(bf16), wo_ref[...], preferred_element_type=f32)
    hn_ref[...] = hn

    ms = jnp.mean(hn * hn, axis=-1, keepdims=True)
    xn = ((hn * lax.rsqrt(ms + EPS)) * gn_ref[...]).astype(bf16)
    logits = jnp.dot(xn, wr_ref[...], preferred_element_type=f32) + br_ref[...]
    lt_ref[...] = logits.T[:N_EXPERTS, :]
    _store_row_tiles(xp_ref, _pack_halves(xn))


def _merge(h, proj, oa, ob, om, wa, wb, wm, wo, gn, wr, br, layer, *, tm):
    t, d = h.shape
    ng = 3 * d // LANE
    return pl.pallas_call(
        functools.partial(_merge_body, d=d),
        grid=(t // tm,),
        in_specs=[
            pl.BlockSpec((tm, d), lambda i: (i, 0)),
            pl.BlockSpec((ng, tm, LANE), lambda i: (0, i, 0)),
            pl.BlockSpec((N_HEADS_A, tm, LANE), lambda i: (0, i, 0)),
            pl.BlockSpec((N_HEADS_B, tm, LANE), lambda i: (0, i, 0)),
            pl.BlockSpec((N_HEADS_M, tm, LANE), lambda i: (0, i, 0)),
            _layer_spec(wa.shape, layer), _layer_spec(wb.shape, layer), _layer_spec(wm.shape, layer),
            _layer_spec(wo.shape, layer), _layer_spec(gn.shape, layer),
            _const_spec(wr.shape), _const_spec(br.shape),
        ],
        out_specs=[
            pl.BlockSpec((tm, d), lambda i: (i, 0)),
            pl.BlockSpec((tm * (d // 2 // LANE), LANE), lambda i: (i, 0)),
            pl.BlockSpec((N_EXPERTS, tm), lambda i: (0, i)),
        ],
        out_shape=[
            jax.ShapeDtypeStruct((t, d), f32),
            jax.ShapeDtypeStruct((t * (d // 2 // LANE), LANE), jnp.uint32),
            jax.ShapeDtypeStruct((N_EXPERTS, t), f32),
        ],
        compiler_params=_cparams(("parallel",), 56),
        name="merge_out_router",
    )(h, proj, oa, ob, om, wa, wb, wm, wo, gn, wr, br)


def _route_body(lt_ref, idx_ref, wt_ref, be_ref, nu_ref, run_ref, tot_ref, *, tn, nbp):
    phase = pl.program_id(0)
    i = pl.program_id(1)
    last = pl.num_programs(1) - 1

    @pl.when(i == 0)
    def _():
        run_ref[...] = jnp.zeros_like(run_ref)

    x = lt_ref[...]
    ex = jnp.exp(x - x.max(axis=0, keepdims=True))
    sc = ex / ex.sum(axis=0, keepdims=True)

    def top2(rows):
        m1 = functools.reduce(jnp.maximum, rows)
        i1 = jnp.full_like(m1, float(len(rows) - 1))
        for j in range(len(rows) - 2, -1, -1):
            i1 = jnp.where(rows[j] == m1, float(j), i1)
        rest = [jnp.where(i1 == float(j), -1.0, r) for j, r in enumerate(rows)]
        m2 = functools.reduce(jnp.maximum, rest)
        i2 = jnp.full_like(m2, float(len(rows) - 1))
        for j in range(len(rows) - 2, -1, -1):
            i2 = jnp.where(rest[j] == m2, float(j), i2)
        return m1, i1, m2, i2

    groups = [top2([sc[g * E_PER_GROUP + j:g * E_PER_GROUP + j + 1, :] for j in range(E_PER_GROUP)])
              for g in range(N_GROUPS)]
    gs = [g[0] + g[2] for g in groups]
    best = functools.reduce(jnp.maximum, gs)
    sel = [groups[N_GROUPS - 1][k] for k in range(4)]
    gi = jnp.full_like(best, float(N_GROUPS - 1))
    for g in range(N_GROUPS - 2, -1, -1):
        hit = gs[g] == best
        sel = [jnp.where(hit, groups[g][k], sel[k]) for k in range(4)]
        gi = jnp.where(hit, float(g), gi)
    m1, i1, m2, i2 = sel
    e0 = gi * E_PER_GROUP + i1
    e1 = gi * E_PER_GROUP + i2
    wsum = m1 + m2
    w0 = m1 / wsum
    w1 = m2 / wsum

    erow = lax.broadcasted_iota(jnp.int32, (N_EXPERTS, 1), 0).astype(f32)
    oh0 = (erow == e0).astype(f32)
    oh1 = (erow == e1).astype(f32)
    sel_mask = oh0 + oh1
    r = lax.broadcasted_iota(jnp.int32, (tn, tn), 0)
    c = lax.broadcasted_iota(jnp.int32, (tn, tn), 1)
    before = (r < c).astype(bf16)
    cnt = jnp.dot(sel_mask.astype(bf16), before, preferred_element_type=f32) + run_ref[:, 0:1]
    run_ref[...] = run_ref[...] + sel_mask.sum(axis=1, keepdims=True)

    @pl.when(jnp.logical_and(phase == 0, i == last))
    def _():
        tot_ref[...] = run_ref[...]

    @pl.when(phase == 1)
    def _():
        tot = tot_ref[:, 0:1]
        padded = jnp.floor((tot + (MOE_BLOCK - 1)) / MOE_BLOCK) * MOE_BLOCK
        start = jnp.zeros_like(padded)
        for e in range(N_EXPERTS - 1):
            start = start + jnp.where(erow > float(e), padded[e:e + 1, :], 0.0)
        slot = start + cnt
        d0 = (oh0 * slot).sum(axis=0, keepdims=True)
        d1 = (oh1 * slot).sum(axis=0, keepdims=True)
        r8 = lax.broadcasted_iota(jnp.int32, (8, 1), 0)
        rows = jnp.where(r8 == 0, e0, jnp.where(r8 == 1, e1, jnp.where(r8 == 2, d0, jnp.where(r8 == 3, d1, 0.0))))
        idx_ref[...] = rows.astype(jnp.int32)
        rl = lax.broadcasted_iota(jnp.int32, (LANE, 1), 0)
        wt_ref[...] = jnp.where(rl == 0, w0, jnp.where(rl == 1, w1, 0.0)).T
        end = start + padded
        blk = lax.broadcasted_iota(jnp.int32, (1, nbp), 1).astype(f32) * MOE_BLOCK
        be = (end <= blk).astype(f32).sum(axis=0, keepdims=True)
        be_ref[...] = jnp.minimum(be, N_EXPERTS - 1.0).astype(jnp.int32)
        nu_ref[...] = jnp.broadcast_to(end[N_EXPERTS - 1:, :] / MOE_BLOCK, (1, LANE)).astype(jnp.int32)


def _route(lt, *, tn, nbp):
    t = lt.shape[1]
    return pl.pallas_call(
        functools.partial(_route_body, tn=tn, nbp=nbp),
        grid=(2, t // tn),
        in_specs=[pl.BlockSpec((N_EXPERTS, tn), lambda p, i: (0, i))],
        out_specs=[
            pl.BlockSpec((8, tn), lambda p, i: (0, i * p)),
            pl.BlockSpec((tn, LANE), lambda p, i: (i * p, 0)),
            pl.BlockSpec((1, nbp), lambda p, i: (0, 0)),
            pl.BlockSpec((1, LANE), lambda p, i: (0, 0)),
        ],
        out_shape=[
            jax.ShapeDtypeStruct((8, t), jnp.int32),
            jax.ShapeDtypeStruct((t, LANE), f32),
            jax.ShapeDtypeStruct((1, nbp), jnp.int32),
            jax.ShapeDtypeStruct((1, LANE), jnp.int32),
        ],
        scratch_shapes=[pltpu.VMEM((N_EXPERTS, LANE), f32), pltpu.VMEM((N_EXPERTS, LANE), f32)],
        compiler_params=_cparams(("arbitrary", "arbitrary"), 32),
        name="route",
    )(lt)


def _dispatch_body(dest_ref, x_ref, prev_ref, xb_ref, sem, *, tc, t, rt):
    del prev_ref
    base = pl.program_id(0) * tc

    def copy(r, d):
        return pltpu.make_async_copy(x_ref.at[pl.ds(pl.multiple_of(r * rt, rt), rt), :],
                                     xb_ref.at[pl.ds(pl.multiple_of(d * rt, rt), rt), :], sem)

    def issue(g, carry):
        for u in range(SUBLANES):
            r = g * SUBLANES + u
            copy(r, dest_ref[base + r]).start()
            copy(r, dest_ref[t + base + r]).start()
        return carry

    lax.fori_loop(0, tc // SUBLANES, issue, 0)

    def drain(g, carry):
        for _ in range(2 * SUBLANES):
            copy(0, 0).wait()
        return carry

    lax.fori_loop(0, tc // SUBLANES, drain, 0)


def _dispatch(dest, xp, slots, *, tc, rt):
    t = xp.shape[0] // rt
    grid_spec = pltpu.PrefetchScalarGridSpec(
        num_scalar_prefetch=1,
        grid=(t // tc,),
        in_specs=[
            pl.BlockSpec((tc * rt, LANE), lambda i, dest: (i, 0)),
            pl.BlockSpec(memory_space=pl.ANY),
        ],
        out_specs=pl.BlockSpec(memory_space=pl.ANY),
        scratch_shapes=[pltpu.SemaphoreType.DMA],
    )
    return pl.pallas_call(
        functools.partial(_dispatch_body, tc=tc, t=t, rt=rt),
        grid_spec=grid_spec,
        out_shape=jax.ShapeDtypeStruct(slots.shape, slots.dtype),
        input_output_aliases={2: 0},
        compiler_params=_cparams(("arbitrary",), 32),
        name="dispatch",
    )(dest, xp, slots)


def _ffn_body(be_ref, nu_ref, x_ref, w1_ref, w3_ref, w2_ref, y_ref, *, dh):
    del be_ref
    b = pl.program_id(0)

    @pl.when(b < nu_ref[0])
    def _():
        tiles = [_unpack_halves(u) for u in _load_row_tiles(x_ref, MOE_BLOCK)]
        lo = jnp.concatenate([p[0] for p in tiles], axis=-1).astype(bf16)
        hi = jnp.concatenate([p[1] for p in tiles], axis=-1).astype(bf16)
        h1 = (jnp.dot(lo, w1_ref[:dh, :], preferred_element_type=f32)
              + jnp.dot(hi, w1_ref[dh:, :], preferred_element_type=f32))
        h3 = (jnp.dot(lo, w3_ref[:dh, :], preferred_element_type=f32)
              + jnp.dot(hi, w3_ref[dh:, :], preferred_element_type=f32))
        hid = (h1 / (1.0 + jnp.exp(-h1))) * h3
        y = jnp.dot(hid.astype(bf16), w2_ref[...], preferred_element_type=f32)
        _store_row_tiles(y_ref, _pack_halves(y))

    @pl.when(b >= nu_ref[0])
    def _():
        y_ref[...] = jnp.zeros_like(y_ref)


def _ffn(block_e, n_used, xb, w1, w3, w2, layer):
    _, _, d, f = w1.shape
    dh = d // 2
    rt = dh // LANE
    nb = xb.shape[0] // (MOE_BLOCK * rt)
    grid_spec = pltpu.PrefetchScalarGridSpec(
        num_scalar_prefetch=2,
        grid=(nb,),
        in_specs=[
            pl.BlockSpec((MOE_BLOCK * rt, LANE), lambda b, be, nu: (b, 0)),
            pl.BlockSpec((None, None, d, f), lambda b, be, nu: (layer, be[b], 0, 0)),
            pl.BlockSpec((None, None, d, f), lambda b, be, nu: (layer, be[b], 0, 0)),
            pl.BlockSpec((None, None, f, d), lambda b, be, nu: (layer, be[b], 0, 0)),
        ],
        out_specs=pl.BlockSpec((MOE_BLOCK * rt, LANE), lambda b, be, nu: (b, 0)),
    )
    return pl.pallas_call(
        functools.partial(_ffn_body, dh=dh),
        grid_spec=grid_spec,
        out_shape=jax.ShapeDtypeStruct(xb.shape, jnp.uint32),
        compiler_params=_cparams(("arbitrary",), 56),
        name="expert_ffn",
    )(block_e, n_used, xb, w1, w3, w2)


def _combine_body(dest_ref, h_ref, wt_ref, g_ref, y_ref, o_ref, buf0, buf1, sem, *, tc, t, rt, final):
    base = pl.program_id(0) * tc

    def copy(d, buf, r):
        return pltpu.make_async_copy(y_ref.at[pl.ds(pl.multiple_of(d * rt, rt), rt), :],
                                     buf.at[pl.ds(pl.multiple_of(r * rt, rt), rt), :], sem)

    def issue(g, carry):
        for u in range(SUBLANES):
            r = g * SUBLANES + u
            copy(dest_ref[base + r], buf0, r).start()
            copy(dest_ref[t + base + r], buf1, r).start()
        return carry

    lax.fori_loop(0, tc // SUBLANES, issue, 0)

    def drain(g, carry):
        for _ in range(SUBLANES):
            copy(0, buf0, 0).wait()
            copy(0, buf1, 0).wait()
        return carry

    lax.fori_loop(0, tc // SUBLANES, drain, 0)
    w = wt_ref[...]
    w0, w1 = w[:, 0:1], w[:, 1:2]
    dh = rt * LANE
    ss = jnp.zeros((tc, 1), f32)
    for k, (u0, u1) in enumerate(zip(_load_row_tiles(buf0, tc), _load_row_tiles(buf1, tc))):
        lo0, hi0 = _unpack_halves(u0)
        lo1, hi1 = _unpack_halves(u1)
        for cols, y0, y1 in ((slice(k * LANE, (k + 1) * LANE), lo0, lo1),
                             (slice(dh + k * LANE, dh + (k + 1) * LANE), hi0, hi1)):
            out = h_ref[:, cols] + (w0 * y0 + w1 * y1)
            o_ref[:, cols] = out
            if final:
                ss = ss + (out * out).sum(axis=-1, keepdims=True)
    if final:
        inv = lax.rsqrt(ss / (2 * dh) + EPS)
        o_ref[...] = (o_ref[...] * inv) * g_ref[...]


def _combine(dest, h, wt, g, y, *, tc, rt, final):
    t, d = h.shape
    grid_spec = pltpu.PrefetchScalarGridSpec(
        num_scalar_prefetch=1,
        grid=(t // tc,),
        in_specs=[
            pl.BlockSpec((tc, d), lambda i, dest: (i, 0)),
            pl.BlockSpec((tc, LANE), lambda i, dest: (i, 0)),
            pl.BlockSpec((1, d), lambda i, dest: (0, 0)),
            pl.BlockSpec(memory_space=pl.ANY),
        ],
        out_specs=pl.BlockSpec((tc, d), lambda i, dest: (i, 0)),
        scratch_shapes=[pltpu.VMEM((tc * rt, LANE), jnp.uint32), pltpu.VMEM((tc * rt, LANE), jnp.uint32),
                        pltpu.SemaphoreType.DMA],
    )
    return pl.pallas_call(
        functools.partial(_combine_body, tc=tc, t=t, rt=rt, final=final),
        grid_spec=grid_spec,
        out_shape=jax.ShapeDtypeStruct((t, d), f32),
        compiler_params=_cparams(("arbitrary",), 48),
        name="combine",
    )(dest, h, wt, g, y)


def _rel_bias_tables(rel_bias):
    depth, heads, _ = rel_bias.shape
    band0 = (BAND_BLOCKS - 1) * QA_BLOCK
    ncol = (2 * BAND_BLOCKS - 1) * QA_BLOCK
    n = ncol + QA_BLOCK
    rb = rel_bias.astype(f32)
    n_lo = band0 - REL_CLIP + QA_BLOCK - 1
    w = jnp.concatenate([
        jnp.broadcast_to(rb[..., :1], (depth, heads, n_lo)), rb,
        jnp.broadcast_to(rb[..., -1:], (depth, heads, n + 1 - n_lo - rb.shape[-1]))], axis=-1)
    skew = jnp.broadcast_to(w[:, :, None, :], (depth, heads, QA_BLOCK, n + 1))
    skew = skew.reshape(depth, heads, QA_BLOCK * (n + 1))[..., :QA_BLOCK * n]
    tab = skew.reshape(depth, heads, QA_BLOCK, n)[..., QA_BLOCK - 1:QA_BLOCK - 1 + ncol]
    row = jnp.arange(QA_BLOCK)[:, None]
    u = jnp.arange(ncol)[None, :]
    kc = u // CHUNK
    qc = row // CHUNK + BAND_CHUNKS
    valid = (kc >= qc - BAND_CHUNKS) & (kc <= qc)
    tab = jnp.where(valid, tab * LOG2E, NEG)
    return tab.reshape(depth, heads, QA_BLOCK, 2 * BAND_BLOCKS - 1, QA_BLOCK).transpose(0, 1, 3, 2, 4)


def _pad_lanes(w):
    pad = [(0, 0)] * (w.ndim - 1) + [(0, LANE - w.shape[-1])]
    return jnp.pad(w, pad)


def kernel(x, mem, norm_mix, w_in, b_forget, rel_bias, norm_mem, w_mem_kv, w_br_a, w_br_b, w_br_m,
           w_out, norm_ffn, w_router, b_router, w1, w3, w2, norm_final):
    batch, seq, d = x.shape
    n_mem = mem.shape[1]
    depth = w_in.shape[0]
    t = batch * seq
    ng = 3 * d // LANE
    assert ng % N_HEADS_A == 0 and (ng + N_QKV_BLOCKS) % N_HEADS_M == 0
    assert seq % 512 == 0 and n_mem % LANE == 0

    tn_proj = min(1024, math.gcd(3 * d, OFF_FB + W_M))
    tm_proj = min(1024, t)
    tq_b = 512
    tc = min(512, t)
    n_slots = (-(-(t * 2) // MOE_BLOCK) + N_EXPERTS) * MOE_BLOCK
    nb = n_slots // MOE_BLOCK
    nbp = -(-nb // LANE) * LANE

    w_main, w_f = _prep_w_in(w_in.astype(bf16), tk=min(256, d))
    g_mix = norm_mix.reshape(depth, 1, d)
    g_mem = norm_mem.reshape(depth, 1, d)
    g_ffn = norm_ffn.reshape(depth, 1, d)
    g_final = norm_final.reshape(1, d)
    bias_f = _pad_lanes(b_forget.astype(f32)).reshape(depth, 1, LANE)
    w_kv = w_mem_kv.astype(bf16)
    zeros_f = jnp.zeros((depth, d, LANE), bf16)
    tables = _rel_bias_tables(rel_bias)
    wa, wb, wm, wo = (w.astype(bf16) for w in (w_br_a, w_br_b, w_br_m, w_out))
    wr = _pad_lanes(w_router).astype(bf16)
    br = _pad_lanes(b_router.reshape(1, -1)).astype(f32)
    w1b, w3b, w2b = (w.astype(bf16) for w in (w1, w3, w2))

    h = x.reshape(t, d)
    memf = mem.reshape(batch * n_mem, d)
    rt = d // 2 // LANE
    xb = jnp.zeros((n_slots * rt, LANE), jnp.uint32)

    for l in range(depth):
        proj, flog = _norm_proj(h, g_mix, w_main, w_f, l, n_sig_cols=3 * d, tm=tm_proj, tn=tn_proj)
        qaug, kaug = _forget_cumsum(flog, bias_f, l, batch=batch, seq=seq, tb=tq_b)
        kv, _ = _norm_proj(memf, g_mem, w_kv, zeros_f, l, n_sig_cols=0,
                           tm=min(1024, batch * n_mem), tn=W_M)

        oa = _attn_a(proj, tables, l, batch=batch, seq=seq, ng=ng)
        ob = _attn_b(proj, qaug, kaug, batch=batch, seq=seq, ng=ng, tq=tq_b)
        om = _attn_m(proj, kv, batch=batch, seq=seq, n_mem=n_mem, ng=ng, tq=512)

        h, xp, lt = _merge(h, proj, oa, ob, om, wa, wb, wm, wo, g_ffn, wr, br, l, tm=256)

        idx, wt, block_e, n_used = _route(lt, tn=512, nbp=nbp)
        dest = idx[2:4].reshape(2 * t)
        xb = _dispatch(dest, xp, xb, tc=tc, rt=rt)
        yb = _ffn(block_e[0, :nb], n_used[0, :1], xb, w1b, w3b, w2b, l)
        h = _combine(dest, h, wt, g_final, yb, tc=tc, rt=rt, final=(l == depth - 1))

    return h.reshape(batch, seq, d)
```

```python
import functools
import math

import jax
import jax.numpy as jnp
from jax import lax
from jax.experimental import pallas as pl
from jax.experimental.pallas import tpu as pltpu

CHUNK = 64
HEAD_DIM = 128
N_HEADS_A = 6
N_HEADS_B = 6
N_HEADS_M = 4
BAND_CHUNKS = 8
REL_CLIP = 128
N_EXPERTS = 16
N_GROUPS = 4
E_PER_GROUP = 4
MOE_BLOCK = 256
EPS = 1e-6
SCALE = HEAD_DIM ** -0.5
LOG2E = 1.4426950408889634

LANE = 128
SUBLANES = 8
BF16_ROWS = 16
NEG = -1e30
QA_BLOCK = 4 * CHUNK
BAND_BLOCKS = BAND_CHUNKS * CHUNK // QA_BLOCK + 1
N_QKV_BLOCKS = 3 * N_HEADS_A + 3 * N_HEADS_B
W_A = N_HEADS_A * HEAD_DIM
W_B = N_HEADS_B * HEAD_DIM
W_M = N_HEADS_M * HEAD_DIM
OFF_FB = 3 * W_A + 3 * W_B
OFF_QM = OFF_FB + N_HEADS_B
OFF_G = OFF_QM + W_M

f32 = jnp.float32
bf16 = jnp.bfloat16


def _cparams(sem, vmem_mb):
    return pltpu.CompilerParams(dimension_semantics=sem, vmem_limit_bytes=vmem_mb * 1024 * 1024)


def _layer_spec(shape, layer):
    nd = len(shape) - 1
    return pl.BlockSpec((None,) + tuple(shape[1:]), lambda *_: (layer,) + (0,) * nd,
                        pipeline_mode=pl.Buffered(1))


def _pack_halves(x):
    n = x.shape[1] // 2
    xf = x.astype(bf16).astype(f32)
    lo = pltpu.bitcast(xf[:, :n], jnp.uint32)
    hi = pltpu.bitcast(xf[:, n:], jnp.uint32)
    return lax.shift_right_logical(lo, jnp.uint32(16)) | (hi & jnp.uint32(0xFFFF0000))


def _unpack_halves(u):
    lo = pltpu.bitcast(lax.shift_left(u, jnp.uint32(16)), f32)
    hi = pltpu.bitcast(u & jnp.uint32(0xFFFF0000), f32)
    return lo, hi


def _store_row_tiles(ref, packed):
    m, w = packed.shape
    r = w // LANE
    for k in range(r):
        ref[pl.ds(k, m, stride=r), :] = packed[:, k * LANE:(k + 1) * LANE]


def _load_row_tiles(ref, m):
    r = ref.shape[0] // m
    return [ref[pl.ds(k, m, stride=r), :] for k in range(r)]


def _const_spec(shape):
    nd = len(shape)
    return pl.BlockSpec(shape, lambda *_: (0,) * nd, pipeline_mode=pl.Buffered(1))


def _prep_body(w_ref, o_ref, f_ref, *, d):
    o_ref[:, :3 * d] = w_ref[:, OFF_G:OFF_G + 3 * d].astype(bf16)
    o_ref[:, 3 * d:3 * d + OFF_FB] = w_ref[:, :OFF_FB].astype(bf16)
    o_ref[:, 3 * d + OFF_FB:] = w_ref[:, OFF_QM:OFF_G].astype(bf16)
    lane = lax.broadcasted_iota(jnp.int32, (1, LANE), 1)
    f_ref[...] = jnp.where(lane < N_HEADS_B, w_ref[:, OFF_FB:OFF_FB + LANE], 0.0).astype(bf16)


def _prep_w_in(w_in, *, tk):
    depth, d, n_in = w_in.shape
    nm = 3 * d + OFF_FB + W_M
    return pl.pallas_call(
        functools.partial(_prep_body, d=d),
        grid=(depth, d // tk),
        in_specs=[pl.BlockSpec((None, tk, n_in), lambda l, k: (l, k, 0))],
        out_specs=[
            pl.BlockSpec((None, tk, nm), lambda l, k: (l, k, 0)),
            pl.BlockSpec((None, tk, LANE), lambda l, k: (l, k, 0)),
        ],
        out_shape=[
            jax.ShapeDtypeStruct((depth, d, nm), bf16),
            jax.ShapeDtypeStruct((depth, d, LANE), bf16),
        ],
        compiler_params=_cparams(("parallel", "parallel"), 48),
        name="prep_w_in",
    )(w_in)


def _proj_body(x_ref, g_ref, w_ref, wf_ref, o_ref, f_ref, xn_ref, *, n_sig, ncb):
    j = pl.program_id(1)

    @pl.when(j == 0)
    def _():
        x = x_ref[...]
        ms = jnp.mean(x * x, axis=-1, keepdims=True)
        xn = ((x * lax.rsqrt(ms + EPS)) * g_ref[...]).astype(bf16)
        xn_ref[...] = xn
        f_ref[...] = jnp.dot(xn, wf_ref[...], preferred_element_type=f32)

    acc = jnp.dot(xn_ref[...], w_ref[...], preferred_element_type=f32)
    for c in range(ncb):
        a = acc[:, c * LANE:(c + 1) * LANE]
        if n_sig > 0:
            a = jnp.where(j < n_sig, 1.0 / (1.0 + jnp.exp(-a)), a)
        o_ref[c] = a.astype(bf16)


def _norm_proj(x, g, w, wf, layer, *, n_sig_cols, tm, tn):
    m, d = x.shape
    n = w.shape[2]
    ncb = tn // LANE
    return pl.pallas_call(
        functools.partial(_proj_body, n_sig=n_sig_cols // tn, ncb=ncb),
        grid=(m // tm, n // tn),
        in_specs=[
            pl.BlockSpec((tm, d), lambda i, j: (i, 0)),
            pl.BlockSpec((None, 1, d), lambda i, j: (layer, 0, 0)),
            pl.BlockSpec((None, d, tn), lambda i, j: (layer, 0, j)),
            pl.BlockSpec((None, d, LANE), lambda i, j: (layer, 0, 0)),
        ],
        out_specs=[
            pl.BlockSpec((ncb, tm, LANE), lambda i, j: (j, i, 0)),
            pl.BlockSpec((tm, LANE), lambda i, j: (i, 0)),
        ],
        out_shape=[
            jax.ShapeDtypeStruct((n // LANE, m, LANE), bf16),
            jax.ShapeDtypeStruct((m, LANE), f32),
        ],
        scratch_shapes=[pltpu.VMEM((tm, d), bf16)],
        compiler_params=_cparams(("parallel", "arbitrary"), 56),
        name="norm_proj",
    )(x, g, w, wf)


def _split3(x):
    hi = x.astype(bf16)
    r1 = x - hi.astype(f32)
    mid = r1.astype(bf16)
    lo = (r1 - mid.astype(f32)).astype(bf16)
    return hi, mid, lo


def _fcum_body(fl_ref, b_ref, qa_ref, ka_ref, carry_ref, *, tb):
    @pl.when(pl.program_id(1) == 0)
    def _():
        carry_ref[...] = jnp.zeros_like(carry_ref)

    x = fl_ref[...] + b_ref[...]
    lf = jnp.minimum(x, 0.0) - jnp.log1p(jnp.exp(-jnp.abs(x)))
    hi, mid, lo = _split3(lf)
    row = lax.broadcasted_iota(jnp.int32, (tb, tb), 0)
    col = lax.broadcasted_iota(jnp.int32, (tb, tb), 1)
    tri = (col <= row).astype(bf16)
    c = (jnp.dot(tri, hi, preferred_element_type=f32)
         + jnp.dot(tri, mid, preferred_element_type=f32)
         + jnp.dot(tri, lo, preferred_element_type=f32))
    c = c + carry_ref[...]
    carry_ref[...] = c[tb - 1:tb, :]

    lane = lax.broadcasted_iota(jnp.int32, (1, LANE), 1)
    for h in range(N_HEADS_B):
        fh, fm, fl = (p.astype(f32) for p in _split3(c[:, h:h + 1] * LOG2E))
        qa = jnp.where(lane == 0, fh, jnp.where(lane == 1, fm, jnp.where(lane == 2, fl,
                       jnp.where(lane < 6, 1.0, 0.0))))
        ka = jnp.where(lane < 3, 1.0, jnp.where(lane == 3, -fh, jnp.where(lane == 4, -fm,
                       jnp.where(lane == 5, -fl, 0.0))))
        qa_ref[h] = qa.astype(bf16)
        ka_ref[h] = ka.astype(bf16)


def _forget_cumsum(flog, bias, layer, *, batch, seq, tb):
    nsb = seq // tb
    t = batch * seq
    return pl.pallas_call(
        functools.partial(_fcum_body, tb=tb),
        grid=(batch, nsb),
        in_specs=[
            pl.BlockSpec((tb, LANE), lambda b, s: (b * nsb + s, 0)),
            pl.BlockSpec((None, 1, LANE), lambda b, s: (layer, 0, 0)),
        ],
        out_specs=[
            pl.BlockSpec((N_HEADS_B, tb, LANE), lambda b, s: (0, b * nsb + s, 0)),
            pl.BlockSpec((N_HEADS_B, tb, LANE), lambda b, s: (0, b * nsb + s, 0)),
        ],
        out_shape=[
            jax.ShapeDtypeStruct((N_HEADS_B, t, LANE), bf16),
            jax.ShapeDtypeStruct((N_HEADS_B, t, LANE), bf16),
        ],
        scratch_shapes=[pltpu.VMEM((1, LANE), f32)],
        compiler_params=_cparams(("parallel", "arbitrary"), 32),
        name="forget_cumsum",
    )(flog, bias)


def _attn_a_body(q_ref, k_ref, v_ref, tb_ref, o_ref):
    i = pl.program_id(1)
    first = jnp.maximum(i - (BAND_BLOCKS - 1), 0)
    cb0 = jnp.maximum((BAND_BLOCKS - 1) - i, 0)
    for h in range(N_HEADS_A):
        q = (q_ref[h].astype(f32) * (SCALE * LOG2E)).astype(bf16)
        s_blocks = []
        for c in range(BAND_BLOCKS):
            ks = pl.multiple_of((first + c) * QA_BLOCK, QA_BLOCK)
            kb = k_ref[h, pl.ds(ks, QA_BLOCK), :]
            s = lax.dot_general(q, kb, (((1,), (1,)), ((), ())), preferred_element_type=f32)
            s_blocks.append(s + tb_ref[h, cb0 + c])
        m = s_blocks[0].max(axis=-1, keepdims=True)
        for s in s_blocks[1:]:
            m = jnp.maximum(m, s.max(axis=-1, keepdims=True))
        l = jnp.zeros_like(m)
        acc = jnp.zeros((QA_BLOCK, HEAD_DIM), f32)
        for c in range(BAND_BLOCKS):
            p = jnp.exp2(s_blocks[c] - m)
            l = l + p.sum(axis=-1, keepdims=True)
            ks = pl.multiple_of((first + c) * QA_BLOCK, QA_BLOCK)
            vb = v_ref[h, pl.ds(ks, QA_BLOCK), :]
            acc = acc + jnp.dot(p.astype(bf16), vb, preferred_element_type=f32)
        o_ref[h] = (acc / l).astype(bf16)


def _attn_a(proj, tables, layer, *, batch, seq, ng):
    nq = seq // QA_BLOCK
    base = ng // N_HEADS_A
    t = batch * seq
    return pl.pallas_call(
        _attn_a_body,
        grid=(batch, nq),
        in_specs=[
            pl.BlockSpec((N_HEADS_A, QA_BLOCK, LANE), lambda b, i: (base, b * nq + i, 0)),
            pl.BlockSpec((N_HEADS_A, seq, LANE), lambda b, i: (base + 1, b, 0)),
            pl.BlockSpec((N_HEADS_A, seq, LANE), lambda b, i: (base + 2, b, 0)),
            _layer_spec(tables.shape, layer),
        ],
        out_specs=pl.BlockSpec((N_HEADS_A, QA_BLOCK, LANE), lambda b, i: (0, b * nq + i, 0)),
        out_shape=jax.ShapeDtypeStruct((N_HEADS_A, t, LANE), bf16),
        compiler_params=_cparams(("parallel", "arbitrary"), 48),
        name="attn_chunk",
    )(proj, proj, proj, tables)


def _attn_b_body(q_ref, k_ref, v_ref, qa_ref, ka_ref, o_ref, *, tq):
    i = pl.program_id(1)
    row = lax.broadcasted_iota(jnp.int32, (tq, tq), 0)
    col = lax.broadcasted_iota(jnp.int32, (tq, tq), 1)
    causal = col <= row

    def one_head(h, qp, kb, carry, masked):
        m, l, acc = carry
        ks = pl.multiple_of(kb * tq, tq)
        kp = jnp.concatenate([k_ref[h, pl.ds(ks, tq), :], ka_ref[h, pl.ds(ks, tq), :]], axis=-1)
        s = lax.dot_general(qp, kp, (((1,), (1,)), ((), ())), preferred_element_type=f32)
        if masked:
            s = jnp.where(causal, s, NEG)
        m_new = jnp.maximum(m, s.max(axis=-1, keepdims=True))
        alpha = jnp.exp2(m - m_new)
        p = jnp.exp2(s - m_new)
        l = alpha * l + p.sum(axis=-1, keepdims=True)
        acc = alpha * acc + jnp.dot(p.astype(bf16), v_ref[h, pl.ds(ks, tq), :],
                                    preferred_element_type=f32)
        return m_new, l, acc

    for h0 in range(0, N_HEADS_B, 3):
        hs = (h0, h0 + 1, h0 + 2)
        qps = [jnp.concatenate([(q_ref[h].astype(f32) * (SCALE * LOG2E)).astype(bf16), qa_ref[h]], axis=-1)
               for h in hs]

        def step(kb, carry, masked):
            return tuple(one_head(h, qp, kb, c, masked) for h, qp, c in zip(hs, qps, carry))

        init = (jnp.full((tq, 1), NEG, f32), jnp.zeros((tq, 1), f32), jnp.zeros((tq, HEAD_DIM), f32))
        carry = lax.fori_loop(0, i, lambda kb, c: step(kb, c, False), (init,) * len(hs))
        for h, (m, l, acc) in zip(hs, step(i, carry, True)):
            o_ref[h] = (acc / l).astype(bf16)


def _attn_b(proj, qaug, kaug, *, batch, seq, ng, tq):
    nq = seq // tq
    base = ng // N_HEADS_B + 3
    t = batch * seq
    return pl.pallas_call(
        functools.partial(_attn_b_body, tq=tq),
        grid=(batch, nq),
        in_specs=[
            pl.BlockSpec((N_HEADS_B, tq, LANE), lambda b, i: (base, b * nq + i, 0)),
            pl.BlockSpec((N_HEADS_B, seq, LANE), lambda b, i: (base + 1, b, 0)),
            pl.BlockSpec((N_HEADS_B, seq, LANE), lambda b, i: (base + 2, b, 0)),
            pl.BlockSpec((N_HEADS_B, tq, LANE), lambda b, i: (0, b * nq + i, 0)),
            pl.BlockSpec((N_HEADS_B, seq, LANE), lambda b, i: (0, b, 0)),
        ],
        out_specs=pl.BlockSpec((N_HEADS_B, tq, LANE), lambda b, i: (0, b * nq + i, 0)),
        out_shape=jax.ShapeDtypeStruct((N_HEADS_B, t, LANE), bf16),
        compiler_params=_cparams(("parallel", "arbitrary"), 56),
        name="attn_forget",
    )(proj, proj, proj, qaug, kaug)


def _attn_m_body(q_ref, k_ref, v_ref, o_ref):
    for h in range(N_HEADS_M):
        s = lax.dot_general(q_ref[h], k_ref[h], (((1,), (1,)), ((), ())), preferred_element_type=f32)
        s = s * SCALE
        m = s.max(axis=-1, keepdims=True)
        p = jnp.exp(s - m)
        l = p.sum(axis=-1, keepdims=True)
        acc = jnp.dot(p.astype(bf16), v_ref[h], preferred_element_type=f32)
        o_ref[h] = (acc / l).astype(bf16)


def _attn_m(proj, kv, *, batch, seq, n_mem, ng, tq):
    nq = seq // tq
    base = (ng + N_QKV_BLOCKS) // N_HEADS_M
    t = batch * seq
    return pl.pallas_call(
        _attn_m_body,
        grid=(batch, nq),
        in_specs=[
            pl.BlockSpec((N_HEADS_M, tq, LANE), lambda b, i: (base, b * nq + i, 0)),
            pl.BlockSpec((N_HEADS_M, n_mem, LANE), lambda b, i: (0, b, 0)),
            pl.BlockSpec((N_HEADS_M, n_mem, LANE), lambda b, i: (1, b, 0)),
        ],
        out_specs=pl.BlockSpec((N_HEADS_M, tq, LANE), lambda b, i: (0, b * nq + i, 0)),
        out_shape=jax.ShapeDtypeStruct((N_HEADS_M, t, LANE), bf16),
        compiler_params=_cparams(("parallel", "arbitrary"), 32),
        name="attn_mem",
    )(proj, kv, kv)


def _merge_body(h_ref, g_ref, oa_ref, ob_ref, om_ref, wa_ref, wb_ref, wm_ref, wo_ref,
                gn_ref, wr_ref, br_ref, hn_ref, xp_ref, lt_ref, *, d):
    ndb = d // LANE

    def heads(ref, n):
        return jnp.concatenate([ref[h] for h in range(n)], axis=-1)

    def gate(br):
        return jnp.concatenate([g_ref[br * ndb + c] for c in range(ndb)], axis=-1).astype(f32)

    o_a = jnp.dot(heads(oa_ref, N_HEADS_A), wa_ref[...], preferred_element_type=f32)
    merged = gate(0) * o_a
    o_b = jnp.dot(heads(ob_ref, N_HEADS_B), wb_ref[...], preferred_element_type=f32)
    merged = merged + gate(1) * o_b
    o_m = jnp.dot(heads(om_ref, N_HEADS_M), wm_ref[...], preferred_element_type=f32)
    merged = merged + gate(2) * o_m
    hn = h_ref[...] + jnp.dot(merged.astype(bf16), wo_ref[...], preferred_element_type=f32)
    hn_ref[...] = hn

    ms = jnp.mean(hn * hn, axis=-1, keepdims=True)
    xn = ((hn * lax.rsqrt(ms + EPS)) * gn_ref[...]).astype(bf16)
    logits = jnp.dot(xn, wr_ref[...], preferred_element_type=f32) + br_ref[...]
    lt_ref[...] = logits.T[:N_EXPERTS, :]
    _store_row_tiles(xp_ref, _pack_halves(xn))


def _merge(h, proj, oa, ob, om, wa, wb, wm, wo, gn, wr, br, layer, *, tm):
    t, d = h.shape
    ng = 3 * d // LANE
    return pl.pallas_call(
        functools.partial(_merge_body, d=d),
        grid=(t // tm,),
        in_specs=[
            pl.BlockSpec((tm, d), lambda i: (i, 0)),
            pl.BlockSpec((ng, tm, LANE), lambda i: (0, i, 0)),
            pl.BlockSpec((N_HEADS_A, tm, LANE), lambda i: (0, i, 0)),
            pl.BlockSpec((N_HEADS_B, tm, LANE), lambda i: (0, i, 0)),
            pl.BlockSpec((N_HEADS_M, tm, LANE), lambda i: (0, i, 0)),
            _layer_spec(wa.shape, layer), _layer_spec(wb.shape, layer), _layer_spec(wm.shape, layer),
            _layer_spec(wo.shape, layer), _layer_spec(gn.shape, layer),
            _const_spec(wr.shape), _const_spec(br.shape),
        ],
        out_specs=[
            pl.BlockSpec((tm, d), lambda i: (i, 0)),
            pl.BlockSpec((tm * (d // 2 // LANE), LANE), lambda i: (i, 0)),
            pl.BlockSpec((N_EXPERTS, tm), lambda i: (0, i)),
        ],
        out_shape=[
            jax.ShapeDtypeStruct((t, d), f32),
            jax.ShapeDtypeStruct((t * (d // 2 // LANE), LANE), jnp.uint32),
            jax.ShapeDtypeStruct((N_EXPERTS, t), f32),
        ],
        compiler_params=_cparams(("parallel",), 56),
        name="merge_out_router",
    )(h, proj, oa, ob, om, wa, wb, wm, wo, gn, wr, br)


def _route_body(lt_ref, idx_ref, wt_ref, be_ref, nu_ref, run_ref, tot_ref, *, tn, nbp):
    phase = pl.program_id(0)
    i = pl.program_id(1)
    last = pl.num_programs(1) - 1

    @pl.when(i == 0)
    def _():
        run_ref[...] = jnp.zeros_like(run_ref)

    x = lt_ref[...]
    ex = jnp.exp(x - x.max(axis=0, keepdims=True))
    sc = ex / ex.sum(axis=0, keepdims=True)

    def top2(rows):
        m1 = functools.reduce(jnp.maximum, rows)
        i1 = jnp.full_like(m1, float(len(rows) - 1))
        for j in range(len(rows) - 2, -1, -1):
            i1 = jnp.where(rows[j] == m1, float(j), i1)
        rest = [jnp.where(i1 == float(j), -1.0, r) for j, r in enumerate(rows)]
        m2 = functools.reduce(jnp.maximum, rest)
        i2 = jnp.full_like(m2, float(len(rows) - 1))
        for j in range(len(rows) - 2, -1, -1):
            i2 = jnp.where(rest[j] == m2, float(j), i2)
        return m1, i1, m2, i2

    groups = [top2([sc[g * E_PER_GROUP + j:g * E_PER_GROUP + j + 1, :] for j in range(E_PER_GROUP)])
              for g in range(N_GROUPS)]
    gs = [g[0] + g[2] for g in groups]
    best = functools.reduce(jnp.maximum, gs)
    sel = [groups[N_GROUPS - 1][k] for k in range(4)]
    gi = jnp.full_like(best, float(N_GROUPS - 1))
    for g in range(N_GROUPS - 2, -1, -1):
        hit = gs[g] == best
        sel = [jnp.where(hit, groups[g][k], sel[k]) for k in range(4)]
        gi = jnp.where(hit, float(g), gi)
    m1, i1, m2, i2 = sel
    e0 = gi * E_PER_GROUP + i1
    e1 = gi * E_PER_GROUP + i2
    wsum = m1 + m2
    w0 = m1 / wsum
    w1 = m2 / wsum

    erow = lax.broadcasted_iota(jnp.int32, (N_EXPERTS, 1), 0).astype(f32)
    oh0 = (erow == e0).astype(f32)
    oh1 = (erow == e1).astype(f32)
    sel_mask = oh0 + oh1
    r = lax.broadcasted_iota(jnp.int32, (tn, tn), 0)
    c = lax.broadcasted_iota(jnp.int32, (tn, tn), 1)
    before = (r < c).astype(bf16)
    cnt = jnp.dot(sel_mask.astype(bf16), before, preferred_element_type=f32) + run_ref[:, 0:1]
    run_ref[...] = run_ref[...] + sel_mask.sum(axis=1, keepdims=True)

    @pl.when(jnp.logical_and(phase == 0, i == last))
    def _():
        tot_ref[...] = run_ref[...]

    @pl.when(phase == 1)
    def _():
        tot = tot_ref[:, 0:1]
        padded = jnp.floor((tot + (MOE_BLOCK - 1)) / MOE_BLOCK) * MOE_BLOCK
        start = jnp.zeros_like(padded)
        for e in range(N_EXPERTS - 1):
            start = start + jnp.where(erow > float(e), padded[e:e + 1, :], 0.0)
        slot = start + cnt
        d0 = (oh0 * slot).sum(axis=0, keepdims=True)
        d1 = (oh1 * slot).sum(axis=0, keepdims=True)
        r8 = lax.broadcasted_iota(jnp.int32, (8, 1), 0)
        rows = jnp.where(r8 == 0, e0, jnp.where(r8 == 1, e1, jnp.where(r8 == 2, d0, jnp.where(r8 == 3, d1, 0.0))))
        idx_ref[...] = rows.astype(jnp.int32)
        rl = lax.broadcasted_iota(jnp.int32, (LANE, 1), 0)
        wt_ref[...] = jnp.where(rl == 0, w0, jnp.where(rl == 1, w1, 0.0)).T
        end = start + padded
        blk = lax.broadcasted_iota(jnp.int32, (1, nbp), 1).astype(f32) * MOE_BLOCK
        be = (end <= blk).astype(f32).sum(axis=0, keepdims=True)
        be_ref[...] = jnp.minimum(be, N_EXPERTS - 1.0).astype(jnp.int32)
        nu_ref[...] = jnp.broadcast_to(end[N_EXPERTS - 1:, :] / MOE_BLOCK, (1, LANE)).astype(jnp.int32)


def _route(lt, *, tn, nbp):
    t = lt.shape[1]
    return pl.pallas_call(
        functools.partial(_route_body, tn=tn, nbp=nbp),
        grid=(2, t // tn),
        in_specs=[pl.BlockSpec((N_EXPERTS, tn), lambda p, i: (0, i))],
        out_specs=[
            pl.BlockSpec((8, tn), lambda p, i: (0, i * p)),
            pl.BlockSpec((tn, LANE), lambda p, i: (i * p, 0)),
            pl.BlockSpec((1, nbp), lambda p, i: (0, 0)),
            pl.BlockSpec((1, LANE), lambda p, i: (0, 0)),
        ],
        out_shape=[
            jax.ShapeDtypeStruct((8, t), jnp.int32),
            jax.ShapeDtypeStruct((t, LANE), f32),
            jax.ShapeDtypeStruct((1, nbp), jnp.int32),
            jax.ShapeDtypeStruct((1, LANE), jnp.int32),
        ],
        scratch_shapes=[pltpu.VMEM((N_EXPERTS, LANE), f32), pltpu.VMEM((N_EXPERTS, LANE), f32)],
        compiler_params=_cparams(("arbitrary", "arbitrary"), 32),
        name="route",
    )(lt)


def _dispatch_body(dest_ref, x_ref, prev_ref, xb_ref, sem, *, tc, t, rt):
    del prev_ref
    base = pl.program_id(0) * tc

    def copy(r, d):
        return pltpu.make_async_copy(x_ref.at[pl.ds(pl.multiple_of(r * rt, rt), rt), :],
                                     xb_ref.at[pl.ds(pl.multiple_of(d * rt, rt), rt), :], sem)

    def issue(g, carry):
        for u in range(SUBLANES):
            r = g * SUBLANES + u
            copy(r, dest_ref[base + r]).start(priority=0)
            copy(r, dest_ref[t + base + r]).start(priority=1)
        return carry

    lax.fori_loop(0, tc // SUBLANES, issue, 0)

    def drain(g, carry):
        for _ in range(2 * SUBLANES):
            copy(0, 0).wait()
        return carry

    lax.fori_loop(0, tc // SUBLANES, drain, 0)


def _dispatch(dest, xp, slots, *, tc, rt):
    t = xp.shape[0] // rt
    grid_spec = pltpu.PrefetchScalarGridSpec(
        num_scalar_prefetch=1,
        grid=(t // tc,),
        in_specs=[
            pl.BlockSpec((tc * rt, LANE), lambda i, dest: (i, 0)),
            pl.BlockSpec(memory_space=pl.ANY),
        ],
        out_specs=pl.BlockSpec(memory_space=pl.ANY),
        scratch_shapes=[pltpu.SemaphoreType.DMA],
    )
    return pl.pallas_call(
        functools.partial(_dispatch_body, tc=tc, t=t, rt=rt),
        grid_spec=grid_spec,
        out_shape=jax.ShapeDtypeStruct(slots.shape, slots.dtype),
        input_output_aliases={2: 0},
        compiler_params=_cparams(("arbitrary",), 32),
        name="dispatch",
    )(dest, xp, slots)


def _ffn_body(be_ref, nu_ref, x_ref, w1_ref, w3_ref, w2_ref, y_ref, *, dh):
    del be_ref
    b = pl.program_id(0)

    @pl.when(b < nu_ref[0])
    def _():
        tiles = [_unpack_halves(u) for u in _load_row_tiles(x_ref, MOE_BLOCK)]
        lo = jnp.concatenate([p[0] for p in tiles], axis=-1).astype(bf16)
        hi = jnp.concatenate([p[1] for p in tiles], axis=-1).astype(bf16)
        h1 = (jnp.dot(lo, w1_ref[:dh, :], preferred_element_type=f32)
              + jnp.dot(hi, w1_ref[dh:, :], preferred_element_type=f32))
        h3 = (jnp.dot(lo, w3_ref[:dh, :], preferred_element_type=f32)
              + jnp.dot(hi, w3_ref[dh:, :], preferred_element_type=f32))
        hid = (h1 / (1.0 + jnp.exp(-h1))) * h3
        y = jnp.dot(hid.astype(bf16), w2_ref[...], preferred_element_type=f32)
        _store_row_tiles(y_ref, _pack_halves(y))

    @pl.when(b >= nu_ref[0])
    def _():
        y_ref[...] = jnp.zeros_like(y_ref)


def _ffn(block_e, n_used, xb, w1, w3, w2, layer):
    _, _, d, f = w1.shape
    dh = d // 2
    rt = dh // LANE
    nb = xb.shape[0] // (MOE_BLOCK * rt)
    grid_spec = pltpu.PrefetchScalarGridSpec(
        num_scalar_prefetch=2,
        grid=(nb,),
        in_specs=[
            pl.BlockSpec((MOE_BLOCK * rt, LANE), lambda b, be, nu: (b, 0)),
            pl.BlockSpec((None, None, d, f), lambda b, be, nu: (layer, be[b], 0, 0)),
            pl.BlockSpec((None, None, d, f), lambda b, be, nu: (layer, be[b], 0, 0)),
            pl.BlockSpec((None, None, f, d), lambda b, be, nu: (layer, be[b], 0, 0)),
        ],
        out_specs=pl.BlockSpec((MOE_BLOCK * rt, LANE), lambda b, be, nu: (b, 0)),
    )
    return pl.pallas_call(
        functools.partial(_ffn_body, dh=dh),
        grid_spec=grid_spec,
        out_shape=jax.ShapeDtypeStruct(xb.shape, jnp.uint32),
        compiler_params=_cparams(("arbitrary",), 56),
        name="expert_ffn",
    )(block_e, n_used, xb, w1, w3, w2)


def _combine_body(dest_ref, h_ref, wt_ref, g_ref, y_ref, o_ref, buf0, buf1, sem, *, tc, t, rt, final):
    base = pl.program_id(0) * tc

    def copy(d, buf, r):
        return pltpu.make_async_copy(y_ref.at[pl.ds(pl.multiple_of(d * rt, rt), rt), :],
                                     buf.at[pl.ds(pl.multiple_of(r * rt, rt), rt), :], sem)

    def issue(g, carry):
        for u in range(SUBLANES):
            r = g * SUBLANES + u
            copy(dest_ref[base + r], buf0, r).start(priority=0)
            copy(dest_ref[t + base + r], buf1, r).start(priority=1)
        return carry

    lax.fori_loop(0, tc // SUBLANES, issue, 0)

    def drain(g, carry):
        for _ in range(SUBLANES):
            copy(0, buf0, 0).wait()
            copy(0, buf1, 0).wait()
        return carry

    lax.fori_loop(0, tc // SUBLANES, drain, 0)
    w = wt_ref[...]
    w0, w1 = w[:, 0:1], w[:, 1:2]
    dh = rt * LANE
    ss = jnp.zeros((tc, 1), f32)
    for k, (u0, u1) in enumerate(zip(_load_row_tiles(buf0, tc), _load_row_tiles(buf1, tc))):
        lo0, hi0 = _unpack_halves(u0)
        lo1, hi1 = _unpack_halves(u1)
        for cols, y0, y1 in ((slice(k * LANE, (k + 1) * LANE), lo0, lo1),
                             (slice(dh + k * LANE, dh + (k + 1) * LANE), hi0, hi1)):
            out = h_ref[:, cols] + (w0 * y0 + w1 * y1)
            o_ref[:, cols] = out
            if final:
                ss = ss + (out * out).sum(axis=-1, keepdims=True)
    if final:
        inv = lax.rsqrt(ss / (2 * dh) + EPS)
        o_ref[...] = (o_ref[...] * inv) * g_ref[...]


def _combine(dest, h, wt, g, y, *, tc, rt, final):
    t, d = h.shape
    grid_spec = pltpu.PrefetchScalarGridSpec(
        num_scalar_prefetch=1,
        grid=(t // tc,),
        in_specs=[
            pl.BlockSpec((tc, d), lambda i, dest: (i, 0)),
            pl.BlockSpec((tc, LANE), lambda i, dest: (i, 0)),
            pl.BlockSpec((1, d), lambda i, dest: (0, 0)),
            pl.BlockSpec(memory_space=pl.ANY),
        ],
        out_specs=pl.BlockSpec((tc, d), lambda i, dest: (i, 0)),
        scratch_shapes=[pltpu.VMEM((tc * rt, LANE), jnp.uint32), pltpu.VMEM((tc * rt, LANE), jnp.uint32),
                        pltpu.SemaphoreType.DMA],
    )
    return pl.pallas_call(
        functools.partial(_combine_body, tc=tc, t=t, rt=rt, final=final),
        grid_spec=grid_spec,
        out_shape=jax.ShapeDtypeStruct((t, d), f32),
        compiler_params=_cparams(("arbitrary",), 48),
        name="combine",
    )(dest, h, wt, g, y)


def _rel_bias_tables(rel_bias):
    depth, heads, _ = rel_bias.shape
    band0 = (BAND_BLOCKS - 1) * QA_BLOCK
    ncol = (2 * BAND_BLOCKS - 1) * QA_BLOCK
    n = ncol + QA_BLOCK
    rb = rel_bias.astype(f32)
    n_lo = band0 - REL_CLIP + QA_BLOCK - 1
    w = jnp.concatenate([
        jnp.broadcast_to(rb[..., :1], (depth, heads, n_lo)), rb,
        jnp.broadcast_to(rb[..., -1:], (depth, heads, n + 1 - n_lo - rb.shape[-1]))], axis=-1)
    skew = jnp.broadcast_to(w[:, :, None, :], (depth, heads, QA_BLOCK, n + 1))
    skew = skew.reshape(depth, heads, QA_BLOCK * (n + 1))[..., :QA_BLOCK * n]
    tab = skew.reshape(depth, heads, QA_BLOCK, n)[..., QA_BLOCK - 1:QA_BLOCK - 1 + ncol]
    row = jnp.arange(QA_BLOCK)[:, None]
    u = jnp.arange(ncol)[None, :]
    kc = u // CHUNK
    qc = row // CHUNK + BAND_CHUNKS
    valid = (kc >= qc - BAND_CHUNKS) & (kc <= qc)
    tab = jnp.where(valid, tab * LOG2E, NEG)
    return tab.reshape(depth, heads, QA_BLOCK, 2 * BAND_BLOCKS - 1, QA_BLOCK).transpose(0, 1, 3, 2, 4)


def _pad_lanes(w):
    pad = [(0, 0)] * (w.ndim - 1) + [(0, LANE - w.shape[-1])]
    return jnp.pad(w, pad)


def kernel(x, mem, norm_mix, w_in, b_forget, rel_bias, norm_mem, w_mem_kv, w_br_a, w_br_b, w_br_m,
           w_out, norm_ffn, w_router, b_router, w1, w3, w2, norm_final):
    batch, seq, d = x.shape
    n_mem = mem.shape[1]
    depth = w_in.shape[0]
    t = batch * seq
    ng = 3 * d // LANE
    assert ng % N_HEADS_A == 0 and (ng + N_QKV_BLOCKS) % N_HEADS_M == 0
    assert seq % 512 == 0 and n_mem % LANE == 0

    tn_proj = min(1024, math.gcd(3 * d, OFF_FB + W_M))
    tm_proj = min(1024, t)
    tq_b = 512
    tc = min(512, t)
    n_slots = (-(-(t * 2) // MOE_BLOCK) + N_EXPERTS) * MOE_BLOCK
    nb = n_slots // MOE_BLOCK
    nbp = -(-nb // LANE) * LANE

    w_main, w_f = _prep_w_in(w_in.astype(bf16), tk=min(256, d))
    g_mix = norm_mix.reshape(depth, 1, d)
    g_mem = norm_mem.reshape(depth, 1, d)
    g_ffn = norm_ffn.reshape(depth, 1, d)
    g_final = norm_final.reshape(1, d)
    bias_f = _pad_lanes(b_forget.astype(f32)).reshape(depth, 1, LANE)
    w_kv = w_mem_kv.astype(bf16)
    zeros_f = jnp.zeros((depth, d, LANE), bf16)
    tables = _rel_bias_tables(rel_bias)
    wa, wb, wm, wo = (w.astype(bf16) for w in (w_br_a, w_br_b, w_br_m, w_out))
    wr = _pad_lanes(w_router).astype(bf16)
    br = _pad_lanes(b_router.reshape(1, -1)).astype(f32)
    w1b, w3b, w2b = (w.astype(bf16) for w in (w1, w3, w2))

    h = x.reshape(t, d)
    memf = mem.reshape(batch * n_mem, d)
    rt = d // 2 // LANE
    xb = jnp.zeros((n_slots * rt, LANE), jnp.uint32)

    for l in range(depth):
        proj, flog = _norm_proj(h, g_mix, w_main, w_f, l, n_sig_cols=3 * d, tm=tm_proj, tn=tn_proj)
        qaug, kaug = _forget_cumsum(flog, bias_f, l, batch=batch, seq=seq, tb=tq_b)
        kv, _ = _norm_proj(memf, g_mem, w_kv, zeros_f, l, n_sig_cols=0,
                           tm=min(1024, batch * n_mem), tn=W_M)

        oa = _attn_a(proj, tables, l, batch=batch, seq=seq, ng=ng)
        ob = _attn_b(proj, qaug, kaug, batch=batch, seq=seq, ng=ng, tq=tq_b)
        om = _attn_m(proj, kv, batch=batch, seq=seq, n_mem=n_mem, ng=ng, tq=512)

        h, xp, lt = _merge(h, proj, oa, ob, om, wa, wb, wm, wo, g_ffn, wr, br, l, tm=256)

        idx, wt, block_e, n_used = _route(lt, tn=512, nbp=nbp)
        dest = idx[2:4].reshape(2 * t)
        xb = _dispatch(dest, xp, xb, tc=tc, rt=rt)
        yb = _ffn(block_e[0, :nb], n_used[0, :1], xb, w1b, w3b, w2b, l)
        h = _combine(dest, h, wt, g_final, yb, tc=tc, rt=rt, final=(l == depth - 1))

    return h.reshape(batch, seq, d)
```

```python
import functools
import math

import jax
import jax.numpy as jnp
from jax import lax
from jax.experimental import pallas as pl
from jax.experimental.pallas import tpu as pltpu

CHUNK = 64
HEAD_DIM = 128
N_HEADS_A = 6
N_HEADS_B = 6
N_HEADS_M = 4
BAND_CHUNKS = 8
REL_CLIP = 128
N_EXPERTS = 16
N_GROUPS = 4
E_PER_GROUP = 4
MOE_BLOCK = 256
EPS = 1e-6
SCALE = HEAD_DIM ** -0.5
LOG2E = 1.4426950408889634

LANE = 128
SUBLANES = 8
BF16_ROWS = 16
NEG = -1e30
QA_BLOCK = 4 * CHUNK
BAND_BLOCKS = BAND_CHUNKS * CHUNK // QA_BLOCK + 1
N_QKV_BLOCKS = 3 * N_HEADS_A + 3 * N_HEADS_B
W_A = N_HEADS_A * HEAD_DIM
W_B = N_HEADS_B * HEAD_DIM
W_M = N_HEADS_M * HEAD_DIM
OFF_FB = 3 * W_A + 3 * W_B
OFF_QM = OFF_FB + N_HEADS_B
OFF_G = OFF_QM + W_M

f32 = jnp.float32
bf16 = jnp.bfloat16


def _cparams(sem, vmem_mb):
    return pltpu.CompilerParams(dimension_semantics=sem, vmem_limit_bytes=vmem_mb * 1024 * 1024)


def _layer_spec(shape, layer):
    nd = len(shape) - 1
    return pl.BlockSpec((None,) + tuple(shape[1:]), lambda *_: (layer,) + (0,) * nd,
                        pipeline_mode=pl.Buffered(1))


def _pack_halves(x):
    n = x.shape[1] // 2
    xf = x.astype(bf16).astype(f32)
    lo = pltpu.bitcast(xf[:, :n], jnp.uint32)
    hi = pltpu.bitcast(xf[:, n:], jnp.uint32)
    return lax.shift_right_logical(lo, jnp.uint32(16)) | (hi & jnp.uint32(0xFFFF0000))


def _unpack_halves(u):
    lo = pltpu.bitcast(lax.shift_left(u, jnp.uint32(16)), f32)
    hi = pltpu.bitcast(u & jnp.uint32(0xFFFF0000), f32)
    return lo, hi


def _store_row_tiles(ref, packed):
    m, w = packed.shape
    r = w // LANE
    for k in range(r):
        ref[pl.ds(k, m, stride=r), :] = packed[:, k * LANE:(k + 1) * LANE]


def _load_row_tiles(ref, m):
    r = ref.shape[0] // m
    return [ref[pl.ds(k, m, stride=r), :] for k in range(r)]


def _const_spec(shape):
    nd = len(shape)
    return pl.BlockSpec(shape, lambda *_: (0,) * nd, pipeline_mode=pl.Buffered(1))


def _prep_body(w_ref, o_ref, f_ref, *, d):
    o_ref[:, :3 * d] = w_ref[:, OFF_G:OFF_G + 3 * d].astype(bf16)
    o_ref[:, 3 * d:3 * d + OFF_FB] = w_ref[:, :OFF_FB].astype(bf16)
    o_ref[:, 3 * d + OFF_FB:] = w_ref[:, OFF_QM:OFF_G].astype(bf16)
    lane = lax.broadcasted_iota(jnp.int32, (1, LANE), 1)
    f_ref[...] = jnp.where(lane < N_HEADS_B, w_ref[:, OFF_FB:OFF_FB + LANE], 0.0).astype(bf16)


def _prep_w_in(w_in, *, tk):
    depth, d, n_in = w_in.shape
    nm = 3 * d + OFF_FB + W_M
    return pl.pallas_call(
        functools.partial(_prep_body, d=d),
        grid=(depth, d // tk),
        in_specs=[pl.BlockSpec((None, tk, n_in), lambda l, k: (l, k, 0))],
        out_specs=[
            pl.BlockSpec((None, tk, nm), lambda l, k: (l, k, 0)),
            pl.BlockSpec((None, tk, LANE), lambda l, k: (l, k, 0)),
        ],
        out_shape=[
            jax.ShapeDtypeStruct((depth, d, nm), bf16),
            jax.ShapeDtypeStruct((depth, d, LANE), bf16),
        ],
        compiler_params=_cparams(("parallel", "parallel"), 48),
        name="prep_w_in",
    )(w_in)


def _proj_body(x_ref, g_ref, w_ref, wf_ref, o_ref, f_ref, xn_ref, *, n_sig, ncb):
    j = pl.program_id(1)

    @pl.when(j == 0)
    def _():
        x = x_ref[...]
        ms = jnp.mean(x * x, axis=-1, keepdims=True)
        xn = ((x * lax.rsqrt(ms + EPS)) * g_ref[...]).astype(bf16)
        xn_ref[...] = xn
        f_ref[...] = jnp.dot(xn, wf_ref[...], preferred_element_type=f32)

    acc = jnp.dot(xn_ref[...], w_ref[...], preferred_element_type=f32)
    for c in range(ncb):
        a = acc[:, c * LANE:(c + 1) * LANE]
        if n_sig > 0:
            a = jnp.where(j < n_sig, 1.0 / (1.0 + jnp.exp(-a)), a)
        o_ref[c] = a.astype(bf16)


def _norm_proj(x, g, w, wf, layer, *, n_sig_cols, tm, tn):
    m, d = x.shape
    n = w.shape[2]
    ncb = tn // LANE
    return pl.pallas_call(
        functools.partial(_proj_body, n_sig=n_sig_cols // tn, ncb=ncb),
        grid=(m // tm, n // tn),
        in_specs=[
            pl.BlockSpec((tm, d), lambda i, j: (i, 0)),
            pl.BlockSpec((None, 1, d), lambda i, j: (layer, 0, 0)),
            pl.BlockSpec((None, d, tn), lambda i, j: (layer, 0, j)),
            pl.BlockSpec((None, d, LANE), lambda i, j: (layer, 0, 0)),
        ],
        out_specs=[
            pl.BlockSpec((ncb, tm, LANE), lambda i, j: (j, i, 0)),
            pl.BlockSpec((tm, LANE), lambda i, j: (i, 0)),
        ],
        out_shape=[
            jax.ShapeDtypeStruct((n // LANE, m, LANE), bf16),
            jax.ShapeDtypeStruct((m, LANE), f32),
        ],
        scratch_shapes=[pltpu.VMEM((tm, d), bf16)],
        compiler_params=_cparams(("parallel", "arbitrary"), 56),
        name="norm_proj",
    )(x, g, w, wf)


def _split3(x):
    hi = x.astype(bf16)
    r1 = x - hi.astype(f32)
    mid = r1.astype(bf16)
    lo = (r1 - mid.astype(f32)).astype(bf16)
    return hi, mid, lo


def _fcum_body(fl_ref, b_ref, qa_ref, ka_ref, carry_ref, *, tb):
    @pl.when(pl.program_id(1) == 0)
    def _():
        carry_ref[...] = jnp.zeros_like(carry_ref)

    x = fl_ref[...] + b_ref[...]
    lf = jnp.minimum(x, 0.0) - jnp.log1p(jnp.exp(-jnp.abs(x)))
    hi, mid, lo = _split3(lf)
    row = lax.broadcasted_iota(jnp.int32, (tb, tb), 0)
    col = lax.broadcasted_iota(jnp.int32, (tb, tb), 1)
    tri = (col <= row).astype(bf16)
    c = (jnp.dot(tri, hi, preferred_element_type=f32)
         + jnp.dot(tri, mid, preferred_element_type=f32)
         + jnp.dot(tri, lo, preferred_element_type=f32))
    c = c + carry_ref[...]
    carry_ref[...] = c[tb - 1:tb, :]

    lane = lax.broadcasted_iota(jnp.int32, (1, LANE), 1)
    for h in range(N_HEADS_B):
        fh, fm, fl = (p.astype(f32) for p in _split3(c[:, h:h + 1] * LOG2E))
        qa = jnp.where(lane == 0, fh, jnp.where(lane == 1, fm, jnp.where(lane == 2, fl,
                       jnp.where(lane < 6, 1.0, 0.0))))
        ka = jnp.where(lane < 3, 1.0, jnp.where(lane == 3, -fh, jnp.where(lane == 4, -fm,
                       jnp.where(lane == 5, -fl, 0.0))))
        qa_ref[h] = qa.astype(bf16)
        ka_ref[h] = ka.astype(bf16)


def _forget_cumsum(flog, bias, layer, *, batch, seq, tb):
    nsb = seq // tb
    t = batch * seq
    return pl.pallas_call(
        functools.partial(_fcum_body, tb=tb),
        grid=(batch, nsb),
        in_specs=[
            pl.BlockSpec((tb, LANE), lambda b, s: (b * nsb + s, 0)),
            pl.BlockSpec((None, 1, LANE), lambda b, s: (layer, 0, 0)),
        ],
        out_specs=[
            pl.BlockSpec((N_HEADS_B, tb, LANE), lambda b, s: (0, b * nsb + s, 0)),
            pl.BlockSpec((N_HEADS_B, tb, LANE), lambda b, s: (0, b * nsb + s, 0)),
        ],
        out_shape=[
            jax.ShapeDtypeStruct((N_HEADS_B, t, LANE), bf16),
            jax.ShapeDtypeStruct((N_HEADS_B, t, LANE), bf16),
        ],
        scratch_shapes=[pltpu.VMEM((1, LANE), f32)],
        compiler_params=_cparams(("parallel", "arbitrary"), 32),
        name="forget_cumsum",
    )(flog, bias)


def _attn_a_body(q_ref, k_ref, v_ref, tb_ref, o_ref):
    i = pl.program_id(1)
    first = jnp.maximum(i - (BAND_BLOCKS - 1), 0)
    cb0 = jnp.maximum((BAND_BLOCKS - 1) - i, 0)
    for h in range(N_HEADS_A):
        q = (q_ref[h].astype(f32) * (SCALE * LOG2E)).astype(bf16)
        s_blocks = []
        for c in range(BAND_BLOCKS):
            ks = pl.multiple_of((first + c) * QA_BLOCK, QA_BLOCK)
            kb = k_ref[h, pl.ds(ks, QA_BLOCK), :]
            s = lax.dot_general(q, kb, (((1,), (1,)), ((), ())), preferred_element_type=f32)
            s_blocks.append(s + tb_ref[h, cb0 + c])
        m = s_blocks[0].max(axis=-1, keepdims=True)
        for s in s_blocks[1:]:
            m = jnp.maximum(m, s.max(axis=-1, keepdims=True))
        l = jnp.zeros_like(m)
        acc = jnp.zeros((QA_BLOCK, HEAD_DIM), f32)
        for c in range(BAND_BLOCKS):
            p = jnp.exp2(s_blocks[c] - m)
            l = l + p.sum(axis=-1, keepdims=True)
            ks = pl.multiple_of((first + c) * QA_BLOCK, QA_BLOCK)
            vb = v_ref[h, pl.ds(ks, QA_BLOCK), :]
            acc = acc + jnp.dot(p.astype(bf16), vb, preferred_element_type=f32)
        o_ref[h] = (acc / l).astype(bf16)


def _attn_a(proj, tables, layer, *, batch, seq, ng):
    nq = seq // QA_BLOCK
    base = ng // N_HEADS_A
    t = batch * seq
    return pl.pallas_call(
        _attn_a_body,
        grid=(batch, nq),
        in_specs=[
            pl.BlockSpec((N_HEADS_A, QA_BLOCK, LANE), lambda b, i: (base, b * nq + i, 0)),
            pl.BlockSpec((N_HEADS_A, seq, LANE), lambda b, i: (base + 1, b, 0)),
            pl.BlockSpec((N_HEADS_A, seq, LANE), lambda b, i: (base + 2, b, 0)),
            _layer_spec(tables.shape, layer),
        ],
        out_specs=pl.BlockSpec((N_HEADS_A, QA_BLOCK, LANE), lambda b, i: (0, b * nq + i, 0)),
        out_shape=jax.ShapeDtypeStruct((N_HEADS_A, t, LANE), bf16),
        compiler_params=_cparams(("parallel", "arbitrary"), 48),
        name="attn_chunk",
    )(proj, proj, proj, tables)


def _attn_b_body(q_ref, k_ref, v_ref, qa_ref, ka_ref, o_ref, *, tq):
    i = pl.program_id(1)
    row = lax.broadcasted_iota(jnp.int32, (tq, tq), 0)
    col = lax.broadcasted_iota(jnp.int32, (tq, tq), 1)
    causal = col <= row

    def one_head(h, qp, kb, carry, masked):
        m, l, acc = carry
        ks = pl.multiple_of(kb * tq, tq)
        kp = jnp.concatenate([k_ref[h, pl.ds(ks, tq), :], ka_ref[h, pl.ds(ks, tq), :]], axis=-1)
        s = lax.dot_general(qp, kp, (((1,), (1,)), ((), ())), preferred_element_type=f32)
        if masked:
            s = jnp.where(causal, s, NEG)
        m_new = jnp.maximum(m, s.max(axis=-1, keepdims=True))
        alpha = jnp.exp2(m - m_new)
        p = jnp.exp2(s - m_new)
        l = alpha * l + p.sum(axis=-1, keepdims=True)
        acc = alpha * acc + jnp.dot(p.astype(bf16), v_ref[h, pl.ds(ks, tq), :],
                                    preferred_element_type=f32)
        return m_new, l, acc

    for h0 in range(0, N_HEADS_B, 3):
        hs = (h0, h0 + 1, h0 + 2)
        qps = [jnp.concatenate([(q_ref[h].astype(f32) * (SCALE * LOG2E)).astype(bf16), qa_ref[h]], axis=-1)
               for h in hs]

        def step(kb, carry, masked):
            return tuple(one_head(h, qp, kb, c, masked) for h, qp, c in zip(hs, qps, carry))

        init = (jnp.full((tq, 1), NEG, f32), jnp.zeros((tq, 1), f32), jnp.zeros((tq, HEAD_DIM), f32))
        carry = lax.fori_loop(0, i, lambda kb, c: step(kb, c, False), (init,) * len(hs))
        for h, (m, l, acc) in zip(hs, step(i, carry, True)):
            o_ref[h] = (acc / l).astype(bf16)


def _attn_b(proj, qaug, kaug, *, batch, seq, ng, tq):
    nq = seq // tq
    base = ng // N_HEADS_B + 3
    t = batch * seq
    return pl.pallas_call(
        functools.partial(_attn_b_body, tq=tq),
        grid=(batch, nq),
        in_specs=[
            pl.BlockSpec((N_HEADS_B, tq, LANE), lambda b, i: (base, b * nq + i, 0)),
            pl.BlockSpec((N_HEADS_B, seq, LANE), lambda b, i: (base + 1, b, 0)),
            pl.BlockSpec((N_HEADS_B, seq, LANE), lambda b, i: (base + 2, b, 0)),
            pl.BlockSpec((N_HEADS_B, tq, LANE), lambda b, i: (0, b * nq + i, 0)),
            pl.BlockSpec((N_HEADS_B, seq, LANE), lambda b, i: (0, b, 0)),
        ],
        out_specs=pl.BlockSpec((N_HEADS_B, tq, LANE), lambda b, i: (0, b * nq + i, 0)),
        out_shape=jax.ShapeDtypeStruct((N_HEADS_B, t, LANE), bf16),
        compiler_params=_cparams(("parallel", "arbitrary"), 56),
        name="attn_forget",
    )(proj, proj, proj, qaug, kaug)


def _attn_m_body(q_ref, k_ref, v_ref, o_ref):
    for h in range(N_HEADS_M):
        s = lax.dot_general(q_ref[h], k_ref[h], (((1,), (1,)), ((), ())), preferred_element_type=f32)
        s = s * SCALE
        m = s.max(axis=-1, keepdims=True)
        p = jnp.exp(s - m)
        l = p.sum(axis=-1, keepdims=True)
        acc = jnp.dot(p.astype(bf16), v_ref[h], preferred_element_type=f32)
        o_ref[h] = (acc / l).astype(bf16)


def _attn_m(proj, kv, *, batch, seq, n_mem, ng, tq):
    nq = seq // tq
    base = (ng + N_QKV_BLOCKS) // N_HEADS_M
    t = batch * seq
    return pl.pallas_call(
        _attn_m_body,
        grid=(batch, nq),
        in_specs=[
            pl.BlockSpec((N_HEADS_M, tq, LANE), lambda b, i: (base, b * nq + i, 0)),
            pl.BlockSpec((N_HEADS_M, n_mem, LANE), lambda b, i: (0, b, 0)),
            pl.BlockSpec((N_HEADS_M, n_mem, LANE), lambda b, i: (1, b, 0)),
        ],
        out_specs=pl.BlockSpec((N_HEADS_M, tq, LANE), lambda b, i: (0, b * nq + i, 0)),
        out_shape=jax.ShapeDtypeStruct((N_HEADS_M, t, LANE), bf16),
        compiler_params=_cparams(("parallel", "arbitrary"), 32),
        name="attn_mem",
    )(proj, kv, kv)


def _merge_body(h_ref, g_ref, oa_ref, ob_ref, om_ref, wa_ref, wb_ref, wm_ref, wo_ref,
                gn_ref, wr_ref, br_ref, hn_ref, xp_ref, lt_ref, *, d):
    ndb = d // LANE

    def heads(ref, n):
        return jnp.concatenate([ref[h] for h in range(n)], axis=-1)

    def gate(br):
        return jnp.concatenate([g_ref[br * ndb + c] for c in range(ndb)], axis=-1).astype(f32)

    o_a = jnp.dot(heads(oa_ref, N_HEADS_A), wa_ref[...], preferred_element_type=f32)
    merged = gate(0) * o_a
    o_b = jnp.dot(heads(ob_ref, N_HEADS_B), wb_ref[...], preferred_element_type=f32)
    merged = merged + gate(1) * o_b
    o_m = jnp.dot(heads(om_ref, N_HEADS_M), wm_ref[...], preferred_element_type=f32)
    merged = merged + gate(2) * o_m
    hn = h_ref[...] + jnp.dot(merged.astype(bf16), wo_ref[...], preferred_element_type=f32)
    hn_ref[...] = hn

    ms = jnp.mean(hn * hn, axis=-1, keepdims=True)
    xn = ((hn * lax.rsqrt(ms + EPS)) * gn_ref[...]).astype(bf16)
    logits = jnp.dot(xn, wr_ref[...], preferred_element_type=f32) + br_ref[...]
    lt_ref[...] = logits.T[:N_EXPERTS, :]
    _store_row_tiles(xp_ref, _pack_halves(xn))


def _merge(h, proj, oa, ob, om, wa, wb, wm, wo, gn, wr, br, layer, *, tm):
    t, d = h.shape
    ng = 3 * d // LANE
    return pl.pallas_call(
        functools.partial(_merge_body, d=d),
        grid=(t // tm,),
        in_specs=[
            pl.BlockSpec((tm, d), lambda i: (i, 0)),
            pl.BlockSpec((ng, tm, LANE), lambda i: (0, i, 0)),
            pl.BlockSpec((N_HEADS_A, tm, LANE), lambda i: (0, i, 0)),
            pl.BlockSpec((N_HEADS_B, tm, LANE), lambda i: (0, i, 0)),
            pl.BlockSpec((N_HEADS_M, tm, LANE), lambda i: (0, i, 0)),
            _layer_spec(wa.shape, layer), _layer_spec(wb.shape, layer), _layer_spec(wm.shape, layer),
            _layer_spec(wo.shape, layer), _layer_spec(gn.shape, layer),
            _const_spec(wr.shape), _const_spec(br.shape),
        ],
        out_specs=[
            pl.BlockSpec((tm, d), lambda i: (i, 0)),
            pl.BlockSpec((tm * (d // 2 // LANE), LANE), lambda i: (i, 0)),
            pl.BlockSpec((N_EXPERTS, tm), lambda i: (0, i)),
        ],
        out_shape=[
            jax.ShapeDtypeStruct((t, d), f32),
            jax.ShapeDtypeStruct((t * (d // 2 // LANE), LANE), jnp.uint32),
            jax.ShapeDtypeStruct((N_EXPERTS, t), f32),
        ],
        compiler_params=_cparams(("parallel",), 56),
        name="merge_out_router",
    )(h, proj, oa, ob, om, wa, wb, wm, wo, gn, wr, br)


def _route_body(lt_ref, idx_ref, wt_ref, be_ref, nu_ref, run_ref, tot_ref, *, tn, nbp):
    phase = pl.program_id(0)
    i = pl.program_id(1)
    last = pl.num_programs(1) - 1

    @pl.when(i == 0)
    def _():
        run_ref[...] = jnp.zeros_like(run_ref)

    x = lt_ref[...]
    ex = jnp.exp(x - x.max(axis=0, keepdims=True))
    sc = ex / ex.sum(axis=0, keepdims=True)

    def top2(rows):
        m1 = functools.reduce(jnp.maximum, rows)
        i1 = jnp.full_like(m1, float(len(rows) - 1))
        for j in range(len(rows) - 2, -1, -1):
            i1 = jnp.where(rows[j] == m1, float(j), i1)
        rest = [jnp.where(i1 == float(j), -1.0, r) for j, r in enumerate(rows)]
        m2 = functools.reduce(jnp.maximum, rest)
        i2 = jnp.full_like(m2, float(len(rows) - 1))
        for j in range(len(rows) - 2, -1, -1):
            i2 = jnp.where(rest[j] == m2, float(j), i2)
        return m1, i1, m2, i2

    groups = [top2([sc[g * E_PER_GROUP + j:g * E_PER_GROUP + j + 1, :] for j in range(E_PER_GROUP)])
              for g in range(N_GROUPS)]
    gs = [g[0] + g[2] for g in groups]
    best = functools.reduce(jnp.maximum, gs)
    sel = [groups[N_GROUPS - 1][k] for k in range(4)]
    gi = jnp.full_like(best, float(N_GROUPS - 1))
    for g in range(N_GROUPS - 2, -1, -1):
        hit = gs[g] == best
        sel = [jnp.where(hit, groups[g][k], sel[k]) for k in range(4)]
        gi = jnp.where(hit, float(g), gi)
    m1, i1, m2, i2 = sel
    e0 = gi * E_PER_GROUP + i1
    e1 = gi * E_PER_GROUP + i2
    wsum = m1 + m2
    w0 = m1 / wsum
    w1 = m2 / wsum

    erow = lax.broadcasted_iota(jnp.int32, (N_EXPERTS, 1), 0).astype(f32)
    oh0 = (erow == e0).astype(f32)
    oh1 = (erow == e1).astype(f32)
    sel_mask = oh0 + oh1
    r = lax.broadcasted_iota(jnp.int32, (tn, tn), 0)
    c = lax.broadcasted_iota(jnp.int32, (tn, tn), 1)
    before = (r < c).astype(bf16)
    cnt = jnp.dot(sel_mask.astype(bf16), before, preferred_element_type=f32) + run_ref[:, 0:1]
    run_ref[...] = run_ref[...] + sel_mask.sum(axis=1, keepdims=True)

    @pl.when(jnp.logical_and(phase == 0, i == last))
    def _():
        tot_ref[...] = run_ref[...]

    @pl.when(phase == 1)
    def _():
        tot = tot_ref[:, 0:1]
        padded = jnp.floor((tot + (MOE_BLOCK - 1)) / MOE_BLOCK) * MOE_BLOCK
        start = jnp.zeros_like(padded)
        for e in range(N_EXPERTS - 1):
            start = start + jnp.where(erow > float(e), padded[e:e + 1, :], 0.0)
        slot = start + cnt
        d0 = (oh0 * slot).sum(axis=0, keepdims=True)
        d1 = (oh1 * slot).sum(axis=0, keepdims=True)
        r8 = lax.broadcasted_iota(jnp.int32, (8, 1), 0)
        rows = jnp.where(r8 == 0, e0, jnp.where(r8 == 1, e1, jnp.where(r8 == 2, d0, jnp.where(r8 == 3, d1, 0.0))))
        idx_ref[...] = rows.astype(jnp.int32)
        rl = lax.broadcasted_iota(jnp.int32, (LANE, 1), 0)
        wt_ref[...] = jnp.where(rl == 0, w0, jnp.where(rl == 1, w1, 0.0)).T
        end = start + padded
        blk = lax.broadcasted_iota(jnp.int32, (1, nbp), 1).astype(f32) * MOE_BLOCK
        be = (end <= blk).astype(f32).sum(axis=0, keepdims=True)
        be_ref[...] = jnp.minimum(be, N_EXPERTS - 1.0).astype(jnp.int32)
        nu_ref[...] = jnp.broadcast_to(end[N_EXPERTS - 1:, :] / MOE_BLOCK, (1, LANE)).astype(jnp.int32)


def _route(lt, *, tn, nbp):
    t = lt.shape[1]
    return pl.pallas_call(
        functools.partial(_route_body, tn=tn, nbp=nbp),
        grid=(2, t // tn),
        in_specs=[pl.BlockSpec((N_EXPERTS, tn), lambda p, i: (0, i))],
        out_specs=[
            pl.BlockSpec((8, tn), lambda p, i: (0, i * p)),
            pl.BlockSpec((tn, LANE), lambda p, i: (i * p, 0)),
            pl.BlockSpec((1, nbp), lambda p, i: (0, 0)),
            pl.BlockSpec((1, LANE), lambda p, i: (0, 0)),
        ],
        out_shape=[
            jax.ShapeDtypeStruct((8, t), jnp.int32),
            jax.ShapeDtypeStruct((t, LANE), f32),
            jax.ShapeDtypeStruct((1, nbp), jnp.int32),
            jax.ShapeDtypeStruct((1, LANE), jnp.int32),
        ],
        scratch_shapes=[pltpu.VMEM((N_EXPERTS, LANE), f32), pltpu.VMEM((N_EXPERTS, LANE), f32)],
        compiler_params=_cparams(("arbitrary", "arbitrary"), 32),
        name="route",
    )(lt)


def _dispatch_body(dest_ref, x_ref, prev_ref, xb_ref, sem, *, tc, t, rt):
    del prev_ref
    base = pl.program_id(0) * tc

    def copy(r, d):
        return pltpu.make_async_copy(x_ref.at[pl.ds(pl.multiple_of(r * rt, rt), rt), :],
                                     xb_ref.at[pl.ds(pl.multiple_of(d * rt, rt), rt), :], sem)

    def issue(g, carry):
        for u in range(SUBLANES):
            r = g * SUBLANES + u
            copy(r, dest_ref[base + r]).start(priority=0)
            copy(r, dest_ref[t + base + r]).start(priority=1)
        return carry

    lax.fori_loop(0, tc // SUBLANES, issue, 0)

    def drain(g, carry):
        for _ in range(2 * SUBLANES):
            copy(0, 0).wait()
        return carry

    lax.fori_loop(0, tc // SUBLANES, drain, 0)


def _dispatch(dest, xp, slots, *, tc, rt):
    t = xp.shape[0] // rt
    grid_spec = pltpu.PrefetchScalarGridSpec(
        num_scalar_prefetch=1,
        grid=(t // tc,),
        in_specs=[
            pl.BlockSpec((tc * rt, LANE), lambda i, dest: (i, 0)),
            pl.BlockSpec(memory_space=pl.ANY),
        ],
        out_specs=pl.BlockSpec(memory_space=pl.ANY),
        scratch_shapes=[pltpu.SemaphoreType.DMA],
    )
    return pl.pallas_call(
        functools.partial(_dispatch_body, tc=tc, t=t, rt=rt),
        grid_spec=grid_spec,
        out_shape=jax.ShapeDtypeStruct(slots.shape, slots.dtype),
        input_output_aliases={2: 0},
        compiler_params=_cparams(("arbitrary",), 32),
        name="dispatch",
    )(dest, xp, slots)


def _ffn_body(be_ref, nu_ref, x_ref, w1_ref, w3_ref, w2_ref, y_ref, *, dh):
    del be_ref
    b = pl.program_id(0)

    @pl.when(b < nu_ref[0])
    def _():
        tiles = [_unpack_halves(u) for u in _load_row_tiles(x_ref, MOE_BLOCK)]
        lo = jnp.concatenate([p[0] for p in tiles], axis=-1).astype(bf16)
        hi = jnp.concatenate([p[1] for p in tiles], axis=-1).astype(bf16)
        h1 = (jnp.dot(lo, w1_ref[:dh, :], preferred_element_type=f32)
              + jnp.dot(hi, w1_ref[dh:, :], preferred_element_type=f32))
        h3 = (jnp.dot(lo, w3_ref[:dh, :], preferred_element_type=f32)
              + jnp.dot(hi, w3_ref[dh:, :], preferred_element_type=f32))
        hid = (h1 / (1.0 + jnp.exp(-h1))) * h3
        y = jnp.dot(hid.astype(bf16), w2_ref[...], preferred_element_type=f32)
        _store_row_tiles(y_ref, _pack_halves(y))

    @pl.when(b >= nu_ref[0])
    def _():
        y_ref[...] = jnp.zeros_like(y_ref)


def _ffn(block_e, n_used, xb, w1, w3, w2, layer):
    _, _, d, f = w1.shape
    dh = d // 2
    rt = dh // LANE
    nb = xb.shape[0] // (MOE_BLOCK * rt)
    grid_spec = pltpu.PrefetchScalarGridSpec(
        num_scalar_prefetch=2,
        grid=(nb,),
        in_specs=[
            pl.BlockSpec((MOE_BLOCK * rt, LANE), lambda b, be, nu: (b, 0)),
            pl.BlockSpec((None, None, d, f), lambda b, be, nu: (layer, be[b], 0, 0)),
            pl.BlockSpec((None, None, d, f), lambda b, be, nu: (layer, be[b], 0, 0)),
            pl.BlockSpec((None, None, f, d), lambda b, be, nu: (layer, be[b], 0, 0)),
        ],
        out_specs=pl.BlockSpec((MOE_BLOCK * rt, LANE), lambda b, be, nu: (b, 0)),
    )
    return pl.pallas_call(
        functools.partial(_ffn_body, dh=dh),
        grid_spec=grid_spec,
        out_shape=jax.ShapeDtypeStruct(xb.shape, jnp.uint32),
        compiler_params=_cparams(("arbitrary",), 56),
        name="expert_ffn",
    )(block_e, n_used, xb, w1, w3, w2)


def _combine_body(dest_ref, h_ref, wt_ref, g_ref, y_ref, o_ref, buf0, buf1, sem, *, tc, t, rt, final):
    i = pl.program_id(0)
    cur = i % 2

    def copy(d, buf, b, r):
        return pltpu.make_async_copy(y_ref.at[pl.ds(pl.multiple_of(d * rt, rt), rt), :],
                                     buf.at[b, pl.ds(pl.multiple_of(r * rt, rt), rt), :], sem.at[b])

    def gather(step, b):
        base = step * tc

        def issue(g, carry):
            for u in range(SUBLANES):
                r = g * SUBLANES + u
                copy(dest_ref[base + r], buf0, b, r).start(priority=0)
                copy(dest_ref[t + base + r], buf1, b, r).start(priority=1)
            return carry

        lax.fori_loop(0, tc // SUBLANES, issue, 0)

    @pl.when(i == 0)
    def _():
        gather(0, 0)

    @pl.when(i + 1 < pl.num_programs(0))
    def _():
        gather(i + 1, 1 - cur)

    def drain(g, carry):
        for _ in range(SUBLANES):
            copy(0, buf0, cur, 0).wait()
            copy(0, buf1, cur, 0).wait()
        return carry

    lax.fori_loop(0, tc // SUBLANES, drain, 0)
    w = wt_ref[...]
    w0, w1 = w[:, 0:1], w[:, 1:2]
    dh = rt * LANE
    ss = jnp.zeros((tc, 1), f32)
    for k, (u0, u1) in enumerate(zip(_load_row_tiles(buf0.at[cur], tc), _load_row_tiles(buf1.at[cur], tc))):
        lo0, hi0 = _unpack_halves(u0)
        lo1, hi1 = _unpack_halves(u1)
        for cols, y0, y1 in ((slice(k * LANE, (k + 1) * LANE), lo0, lo1),
                             (slice(dh + k * LANE, dh + (k + 1) * LANE), hi0, hi1)):
            out = h_ref[:, cols] + (w0 * y0 + w1 * y1)
            o_ref[:, cols] = out
            if final:
                ss = ss + (out * out).sum(axis=-1, keepdims=True)
    if final:
        inv = lax.rsqrt(ss / (2 * dh) + EPS)
        o_ref[...] = (o_ref[...] * inv) * g_ref[...]


def _combine(dest, h, wt, g, y, *, tc, rt, final):
    t, d = h.shape
    grid_spec = pltpu.PrefetchScalarGridSpec(
        num_scalar_prefetch=1,
        grid=(t // tc,),
        in_specs=[
            pl.BlockSpec((tc, d), lambda i, dest: (i, 0)),
            pl.BlockSpec((tc, LANE), lambda i, dest: (i, 0)),
            pl.BlockSpec((1, d), lambda i, dest: (0, 0)),
            pl.BlockSpec(memory_space=pl.ANY),
        ],
        out_specs=pl.BlockSpec((tc, d), lambda i, dest: (i, 0)),
        scratch_shapes=[pltpu.VMEM((2, tc * rt, LANE), jnp.uint32), pltpu.VMEM((2, tc * rt, LANE), jnp.uint32),
                        pltpu.SemaphoreType.DMA((2,))],
    )
    return pl.pallas_call(
        functools.partial(_combine_body, tc=tc, t=t, rt=rt, final=final),
        grid_spec=grid_spec,
        out_shape=jax.ShapeDtypeStruct((t, d), f32),
        compiler_params=_cparams(("arbitrary",), 48),
        name="combine",
    )(dest, h, wt, g, y)


def _rel_bias_tables(rel_bias):
    depth, heads, _ = rel_bias.shape
    band0 = (BAND_BLOCKS - 1) * QA_BLOCK
    ncol = (2 * BAND_BLOCKS - 1) * QA_BLOCK
    n = ncol + QA_BLOCK
    rb = rel_bias.astype(f32)
    n_lo = band0 - REL_CLIP + QA_BLOCK - 1
    w = jnp.concatenate([
        jnp.broadcast_to(rb[..., :1], (depth, heads, n_lo)), rb,
        jnp.broadcast_to(rb[..., -1:], (depth, heads, n + 1 - n_lo - rb.shape[-1]))], axis=-1)
    skew = jnp.broadcast_to(w[:, :, None, :], (depth, heads, QA_BLOCK, n + 1))
    skew = skew.reshape(depth, heads, QA_BLOCK * (n + 1))[..., :QA_BLOCK * n]
    tab = skew.reshape(depth, heads, QA_BLOCK, n)[..., QA_BLOCK - 1:QA_BLOCK - 1 + ncol]
    row = jnp.arange(QA_BLOCK)[:, None]
    u = jnp.arange(ncol)[None, :]
    kc = u // CHUNK
    qc = row // CHUNK + BAND_CHUNKS
    valid = (kc >= qc - BAND_CHUNKS) & (kc <= qc)
    tab = jnp.where(valid, tab * LOG2E, NEG)
    return tab.reshape(depth, heads, QA_BLOCK, 2 * BAND_BLOCKS - 1, QA_BLOCK).transpose(0, 1, 3, 2, 4)


def _pad_lanes(w):
    pad = [(0, 0)] * (w.ndim - 1) + [(0, LANE - w.shape[-1])]
    return jnp.pad(w, pad)


def kernel(x, mem, norm_mix, w_in, b_forget, rel_bias, norm_mem, w_mem_kv, w_br_a, w_br_b, w_br_m,
           w_out, norm_ffn, w_router, b_router, w1, w3, w2, norm_final):
    batch, seq, d = x.shape
    n_mem = mem.shape[1]
    depth = w_in.shape[0]
    t = batch * seq
    ng = 3 * d // LANE
    assert ng % N_HEADS_A == 0 and (ng + N_QKV_BLOCKS) % N_HEADS_M == 0
    assert seq % 512 == 0 and n_mem % LANE == 0

    tn_proj = min(1024, math.gcd(3 * d, OFF_FB + W_M))
    tm_proj = min(1024, t)
    tq_b = 512
    tc = min(512, t)
    n_slots = (-(-(t * 2) // MOE_BLOCK) + N_EXPERTS) * MOE_BLOCK
    nb = n_slots // MOE_BLOCK
    nbp = -(-nb // LANE) * LANE

    w_main, w_f = _prep_w_in(w_in.astype(bf16), tk=min(256, d))
    g_mix = norm_mix.reshape(depth, 1, d)
    g_mem = norm_mem.reshape(depth, 1, d)
    g_ffn = norm_ffn.reshape(depth, 1, d)
    g_final = norm_final.reshape(1, d)
    bias_f = _pad_lanes(b_forget.astype(f32)).reshape(depth, 1, LANE)
    w_kv = w_mem_kv.astype(bf16)
    zeros_f = jnp.zeros((depth, d, LANE), bf16)
    tables = _rel_bias_tables(rel_bias)
    wa, wb, wm, wo = (w.astype(bf16) for w in (w_br_a, w_br_b, w_br_m, w_out))
    wr = _pad_lanes(w_router).astype(bf16)
    br = _pad_lanes(b_router.reshape(1, -1)).astype(f32)
    w1b, w3b, w2b = (w.astype(bf16) for w in (w1, w3, w2))

    h = x.reshape(t, d)
    memf = mem.reshape(batch * n_mem, d)
    rt = d // 2 // LANE
    xb = jnp.zeros((n_slots * rt, LANE), jnp.uint32)

    for l in range(depth):
        proj, flog = _norm_proj(h, g_mix, w_main, w_f, l, n_sig_cols=3 * d, tm=tm_proj, tn=tn_proj)
        qaug, kaug = _forget_cumsum(flog, bias_f, l, batch=batch, seq=seq, tb=tq_b)
        kv, _ = _norm_proj(memf, g_mem, w_kv, zeros_f, l, n_sig_cols=0,
                           tm=min(1024, batch * n_mem), tn=W_M)

        oa = _attn_a(proj, tables, l, batch=batch, seq=seq, ng=ng)
        ob = _attn_b(proj, qaug, kaug, batch=batch, seq=seq, ng=ng, tq=tq_b)
        om = _attn_m(proj, kv, batch=batch, seq=seq, n_mem=n_mem, ng=ng, tq=512)

        h, xp, lt = _merge(h, proj, oa, ob, om, wa, wb, wm, wo, g_ffn, wr, br, l, tm=256)

        idx, wt, block_e, n_used = _route(lt, tn=512, nbp=nbp)
        dest = idx[2:4].reshape(2 * t)
        xb = _dispatch(dest, xp, xb, tc=tc, rt=rt)
        yb = _ffn(block_e[0, :nb], n_used[0, :1], xb, w1b, w3b, w2b, l)
        h = _combine(dest, h, wt, g_final, yb, tc=tc, rt=rt, final=(l == depth - 1))

    return h.reshape(batch, seq, d)
```

```python
import functools
import math

import jax
import jax.numpy as jnp
from jax import lax
from jax.experimental import pallas as pl
from jax.experimental.pallas import tpu as pltpu

CHUNK = 64
HEAD_DIM = 128
N_HEADS_A = 6
N_HEADS_B = 6
N_HEADS_M = 4
BAND_CHUNKS = 8
REL_CLIP = 128
N_EXPERTS = 16
N_GROUPS = 4
E_PER_GROUP = 4
MOE_BLOCK = 256
EPS = 1e-6
SCALE = HEAD_DIM ** -0.5
LOG2E = 1.4426950408889634

LANE = 128
SUBLANES = 8
BF16_ROWS = 16
NEG = -1e30
QA_BLOCK = 4 * CHUNK
BAND_BLOCKS = BAND_CHUNKS * CHUNK // QA_BLOCK + 1
N_QKV_BLOCKS = 3 * N_HEADS_A + 3 * N_HEADS_B
W_A = N_HEADS_A * HEAD_DIM
W_B = N_HEADS_B * HEAD_DIM
W_M = N_HEADS_M * HEAD_DIM
OFF_FB = 3 * W_A + 3 * W_B
OFF_QM = OFF_FB + N_HEADS_B
OFF_G = OFF_QM + W_M

f32 = jnp.float32
bf16 = jnp.bfloat16


def _cparams(sem, vmem_mb):
    return pltpu.CompilerParams(dimension_semantics=sem, vmem_limit_bytes=vmem_mb * 1024 * 1024)


def _layer_spec(shape, layer):
    nd = len(shape) - 1
    return pl.BlockSpec((None,) + tuple(shape[1:]), lambda *_: (layer,) + (0,) * nd,
                        pipeline_mode=pl.Buffered(1))


def _pack_halves(x):
    n = x.shape[1] // 2
    xf = x.astype(bf16).astype(f32)
    lo = pltpu.bitcast(xf[:, :n], jnp.uint32)
    hi = pltpu.bitcast(xf[:, n:], jnp.uint32)
    return lax.shift_right_logical(lo, jnp.uint32(16)) | (hi & jnp.uint32(0xFFFF0000))


def _unpack_halves(u):
    lo = pltpu.bitcast(lax.shift_left(u, jnp.uint32(16)), f32)
    hi = pltpu.bitcast(u & jnp.uint32(0xFFFF0000), f32)
    return lo, hi


def _store_row_tiles(ref, packed):
    m, w = packed.shape
    r = w // LANE
    for k in range(r):
        ref[pl.ds(k, m, stride=r), :] = packed[:, k * LANE:(k + 1) * LANE]


def _load_row_tiles(ref, m):
    r = ref.shape[0] // m
    return [ref[pl.ds(k, m, stride=r), :] for k in range(r)]


def _const_spec(shape):
    nd = len(shape)
    return pl.BlockSpec(shape, lambda *_: (0,) * nd, pipeline_mode=pl.Buffered(1))


def _prep_body(w_ref, o_ref, f_ref, *, d):
    o_ref[:, :3 * d] = w_ref[:, OFF_G:OFF_G + 3 * d].astype(bf16)
    o_ref[:, 3 * d:3 * d + OFF_FB] = w_ref[:, :OFF_FB].astype(bf16)
    o_ref[:, 3 * d + OFF_FB:] = w_ref[:, OFF_QM:OFF_G].astype(bf16)
    lane = lax.broadcasted_iota(jnp.int32, (1, LANE), 1)
    f_ref[...] = jnp.where(lane < N_HEADS_B, w_ref[:, OFF_FB:OFF_FB + LANE], 0.0).astype(bf16)


def _prep_w_in(w_in, *, tk):
    depth, d, n_in = w_in.shape
    nm = 3 * d + OFF_FB + W_M
    return pl.pallas_call(
        functools.partial(_prep_body, d=d),
        grid=(depth, d // tk),
        in_specs=[pl.BlockSpec((None, tk, n_in), lambda l, k: (l, k, 0))],
        out_specs=[
            pl.BlockSpec((None, tk, nm), lambda l, k: (l, k, 0)),
            pl.BlockSpec((None, tk, LANE), lambda l, k: (l, k, 0)),
        ],
        out_shape=[
            jax.ShapeDtypeStruct((depth, d, nm), bf16),
            jax.ShapeDtypeStruct((depth, d, LANE), bf16),
        ],
        compiler_params=_cparams(("parallel", "parallel"), 48),
        name="prep_w_in",
    )(w_in)


def _proj_body(x_ref, g_ref, w_ref, wf_ref, o_ref, f_ref, xn_ref, *, n_sig, ncb):
    j = pl.program_id(1)

    @pl.when(j == 0)
    def _():
        x = x_ref[...]
        ms = jnp.mean(x * x, axis=-1, keepdims=True)
        xn = ((x * lax.rsqrt(ms + EPS)) * g_ref[...]).astype(bf16)
        xn_ref[...] = xn
        f_ref[...] = jnp.dot(xn, wf_ref[...], preferred_element_type=f32)

    acc = jnp.dot(xn_ref[...], w_ref[...], preferred_element_type=f32)
    for c in range(ncb):
        a = acc[:, c * LANE:(c + 1) * LANE]
        if n_sig > 0:
            a = jnp.where(j < n_sig, 1.0 / (1.0 + jnp.exp(-a)), a)
        o_ref[c] = a.astype(bf16)


def _norm_proj(x, g, w, wf, layer, *, n_sig_cols, tm, tn):
    m, d = x.shape
    n = w.shape[2]
    ncb = tn // LANE
    return pl.pallas_call(
        functools.partial(_proj_body, n_sig=n_sig_cols // tn, ncb=ncb),
        grid=(m // tm, n // tn),
        in_specs=[
            pl.BlockSpec((tm, d), lambda i, j: (i, 0)),
            pl.BlockSpec((None, 1, d), lambda i, j: (layer, 0, 0)),
            pl.BlockSpec((None, d, tn), lambda i, j: (layer, 0, j)),
            pl.BlockSpec((None, d, LANE), lambda i, j: (layer, 0, 0)),
        ],
        out_specs=[
            pl.BlockSpec((ncb, tm, LANE), lambda i, j: (j, i, 0)),
            pl.BlockSpec((tm, LANE), lambda i, j: (i, 0)),
        ],
        out_shape=[
            jax.ShapeDtypeStruct((n // LANE, m, LANE), bf16),
            jax.ShapeDtypeStruct((m, LANE), f32),
        ],
        scratch_shapes=[pltpu.VMEM((tm, d), bf16)],
        compiler_params=_cparams(("parallel", "arbitrary"), 56),
        name="norm_proj",
    )(x, g, w, wf)


def _split3(x):
    hi = x.astype(bf16)
    r1 = x - hi.astype(f32)
    mid = r1.astype(bf16)
    lo = (r1 - mid.astype(f32)).astype(bf16)
    return hi, mid, lo


def _fcum_body(fl_ref, b_ref, k_ref, qa_ref, kp_ref, carry_ref, *, tb):
    @pl.when(pl.program_id(1) == 0)
    def _():
        carry_ref[...] = jnp.zeros_like(carry_ref)

    x = fl_ref[...] + b_ref[...]
    lf = jnp.minimum(x, 0.0) - jnp.log1p(jnp.exp(-jnp.abs(x)))
    hi, mid, lo = _split3(lf)
    row = lax.broadcasted_iota(jnp.int32, (tb, tb), 0)
    col = lax.broadcasted_iota(jnp.int32, (tb, tb), 1)
    tri = (col <= row).astype(bf16)
    c = (jnp.dot(tri, hi, preferred_element_type=f32)
         + jnp.dot(tri, mid, preferred_element_type=f32)
         + jnp.dot(tri, lo, preferred_element_type=f32))
    c = c + carry_ref[...]
    carry_ref[...] = c[tb - 1:tb, :]

    lane = lax.broadcasted_iota(jnp.int32, (1, LANE), 1)
    for h in range(N_HEADS_B):
        fh, fm, fl = (p.astype(f32) for p in _split3(c[:, h:h + 1] * LOG2E))
        pieces = jnp.where(lane % 3 == 0, fh, jnp.where(lane % 3 == 1, fm, fl))
        qa = jnp.where(lane < 3, pieces, jnp.where(lane < 6, 1.0, 0.0))
        ka = jnp.where(lane < 3, 1.0, jnp.where(lane < 6, -pieces, 0.0))
        qa_ref[h] = qa.astype(bf16)
        kp_ref[h, :, :HEAD_DIM] = k_ref[h]
        kp_ref[h, :, HEAD_DIM:] = ka.astype(bf16)


def _forget_cumsum(flog, bias, proj, layer, *, batch, seq, tb, ng):
    nsb = seq // tb
    t = batch * seq
    kbase = ng // N_HEADS_B + 4
    return pl.pallas_call(
        functools.partial(_fcum_body, tb=tb),
        grid=(batch, nsb),
        in_specs=[
            pl.BlockSpec((tb, LANE), lambda b, s: (b * nsb + s, 0)),
            pl.BlockSpec((None, 1, LANE), lambda b, s: (layer, 0, 0)),
            pl.BlockSpec((N_HEADS_B, tb, LANE), lambda b, s: (kbase, b * nsb + s, 0)),
        ],
        out_specs=[
            pl.BlockSpec((N_HEADS_B, tb, LANE), lambda b, s: (0, b * nsb + s, 0)),
            pl.BlockSpec((N_HEADS_B, tb, 2 * HEAD_DIM), lambda b, s: (0, b * nsb + s, 0)),
        ],
        out_shape=[
            jax.ShapeDtypeStruct((N_HEADS_B, t, LANE), bf16),
            jax.ShapeDtypeStruct((N_HEADS_B, t, 2 * HEAD_DIM), bf16),
        ],
        scratch_shapes=[pltpu.VMEM((1, LANE), f32)],
        compiler_params=_cparams(("parallel", "arbitrary"), 32),
        name="forget_cumsum",
    )(flog, bias, proj)


def _attn_a_body(q_ref, k_ref, v_ref, tb_ref, o_ref):
    i = pl.program_id(1)
    first = jnp.maximum(i - (BAND_BLOCKS - 1), 0)
    cb0 = jnp.maximum((BAND_BLOCKS - 1) - i, 0)
    for h in range(N_HEADS_A):
        q = (q_ref[h].astype(f32) * (SCALE * LOG2E)).astype(bf16)
        s_blocks = []
        for c in range(BAND_BLOCKS):
            ks = pl.multiple_of((first + c) * QA_BLOCK, QA_BLOCK)
            kb = k_ref[h, pl.ds(ks, QA_BLOCK), :]
            s = lax.dot_general(q, kb, (((1,), (1,)), ((), ())), preferred_element_type=f32)
            s_blocks.append(s + tb_ref[h, cb0 + c])
        m = s_blocks[0].max(axis=-1, keepdims=True)
        for s in s_blocks[1:]:
            m = jnp.maximum(m, s.max(axis=-1, keepdims=True))
        l = jnp.zeros_like(m)
        acc = jnp.zeros((QA_BLOCK, HEAD_DIM), f32)
        for c in range(BAND_BLOCKS):
            p = jnp.exp2(s_blocks[c] - m)
            l = l + p.sum(axis=-1, keepdims=True)
            ks = pl.multiple_of((first + c) * QA_BLOCK, QA_BLOCK)
            vb = v_ref[h, pl.ds(ks, QA_BLOCK), :]
            acc = acc + jnp.dot(p.astype(bf16), vb, preferred_element_type=f32)
        o_ref[h] = (acc / l).astype(bf16)


def _attn_a(proj, tables, layer, *, batch, seq, ng):
    nq = seq // QA_BLOCK
    base = ng // N_HEADS_A
    t = batch * seq
    return pl.pallas_call(
        _attn_a_body,
        grid=(batch, nq),
        in_specs=[
            pl.BlockSpec((N_HEADS_A, QA_BLOCK, LANE), lambda b, i: (base, b * nq + i, 0)),
            pl.BlockSpec((N_HEADS_A, seq, LANE), lambda b, i: (base + 1, b, 0)),
            pl.BlockSpec((N_HEADS_A, seq, LANE), lambda b, i: (base + 2, b, 0)),
            _layer_spec(tables.shape, layer),
        ],
        out_specs=pl.BlockSpec((N_HEADS_A, QA_BLOCK, LANE), lambda b, i: (0, b * nq + i, 0)),
        out_shape=jax.ShapeDtypeStruct((N_HEADS_A, t, LANE), bf16),
        compiler_params=_cparams(("parallel", "arbitrary"), 48),
        name="attn_chunk",
    )(proj, proj, proj, tables)


def _attn_b_body(q_ref, v_ref, qa_ref, kp_ref, o_ref, *, tq):
    i = pl.program_id(1)
    row = lax.broadcasted_iota(jnp.int32, (tq, tq), 0)
    col = lax.broadcasted_iota(jnp.int32, (tq, tq), 1)
    causal = col <= row

    def one_head(h, qp, kb, carry, masked):
        m, l, acc = carry
        ks = pl.multiple_of(kb * tq, tq)
        kp = kp_ref[h, pl.ds(ks, tq), :]
        s = lax.dot_general(qp, kp, (((1,), (1,)), ((), ())), preferred_element_type=f32)
        if masked:
            s = jnp.where(causal, s, NEG)
        m_new = jnp.maximum(m, s.max(axis=-1, keepdims=True))
        alpha = jnp.exp2(m - m_new)
        p = jnp.exp2(s - m_new)
        l = alpha * l + p.sum(axis=-1, keepdims=True)
        acc = alpha * acc + jnp.dot(p.astype(bf16), v_ref[h, pl.ds(ks, tq), :],
                                    preferred_element_type=f32)
        return m_new, l, acc

    for h0 in range(0, N_HEADS_B, 6):
        hs = tuple(range(h0, h0 + 6))
        qps = [jnp.concatenate([(q_ref[h].astype(f32) * (SCALE * LOG2E)).astype(bf16), qa_ref[h]], axis=-1)
               for h in hs]

        def step(kb, carry, masked):
            return tuple(one_head(h, qp, kb, c, masked) for h, qp, c in zip(hs, qps, carry))

        init = (jnp.full((tq, 1), NEG, f32), jnp.zeros((tq, 1), f32), jnp.zeros((tq, HEAD_DIM), f32))
        carry = lax.fori_loop(0, i, lambda kb, c: step(kb, c, False), (init,) * len(hs))
        for h, (m, l, acc) in zip(hs, step(i, carry, True)):
            o_ref[h] = (acc / l).astype(bf16)


def _attn_b(proj, qaug, kp, *, batch, seq, ng, tq):
    nq = seq // tq
    base = ng // N_HEADS_B + 3
    t = batch * seq
    return pl.pallas_call(
        functools.partial(_attn_b_body, tq=tq),
        grid=(batch, nq),
        in_specs=[
            pl.BlockSpec((N_HEADS_B, tq, LANE), lambda b, i: (base, b * nq + i, 0)),
            pl.BlockSpec((N_HEADS_B, seq, LANE), lambda b, i: (base + 2, b, 0)),
            pl.BlockSpec((N_HEADS_B, tq, LANE), lambda b, i: (0, b * nq + i, 0)),
            pl.BlockSpec((N_HEADS_B, seq, 2 * HEAD_DIM), lambda b, i: (0, b, 0)),
        ],
        out_specs=pl.BlockSpec((N_HEADS_B, tq, LANE), lambda b, i: (0, b * nq + i, 0)),
        out_shape=jax.ShapeDtypeStruct((N_HEADS_B, t, LANE), bf16),
        compiler_params=_cparams(("parallel", "arbitrary"), 56),
        name="attn_forget",
    )(proj, proj, qaug, kp)


def _attn_m_body(q_ref, k_ref, v_ref, o_ref):
    for h in range(N_HEADS_M):
        s = lax.dot_general(q_ref[h], k_ref[h], (((1,), (1,)), ((), ())), preferred_element_type=f32)
        s = s * SCALE
        m = s.max(axis=-1, keepdims=True)
        p = jnp.exp(s - m)
        l = p.sum(axis=-1, keepdims=True)
        acc = jnp.dot(p.astype(bf16), v_ref[h], preferred_element_type=f32)
        o_ref[h] = (acc / l).astype(bf16)


def _attn_m(proj, kv, *, batch, seq, n_mem, ng, tq):
    nq = seq // tq
    base = (ng + N_QKV_BLOCKS) // N_HEADS_M
    t = batch * seq
    return pl.pallas_call(
        _attn_m_body,
        grid=(batch, nq),
        in_specs=[
            pl.BlockSpec((N_HEADS_M, tq, LANE), lambda b, i: (base, b * nq + i, 0)),
            pl.BlockSpec((N_HEADS_M, n_mem, LANE), lambda b, i: (0, b, 0)),
            pl.BlockSpec((N_HEADS_M, n_mem, LANE), lambda b, i: (1, b, 0)),
        ],
        out_specs=pl.BlockSpec((N_HEADS_M, tq, LANE), lambda b, i: (0, b * nq + i, 0)),
        out_shape=jax.ShapeDtypeStruct((N_HEADS_M, t, LANE), bf16),
        compiler_params=_cparams(("parallel", "arbitrary"), 32),
        name="attn_mem",
    )(proj, kv, kv)


def _merge_body(h_ref, g_ref, oa_ref, ob_ref, om_ref, wa_ref, wb_ref, wm_ref, wo_ref,
                gn_ref, wr_ref, br_ref, hn_ref, xp_ref, lt_ref, *, d):
    ndb = d // LANE

    def heads(ref, n):
        return jnp.concatenate([ref[h] for h in range(n)], axis=-1)

    def gate(br):
        return jnp.concatenate([g_ref[br * ndb + c] for c in range(ndb)], axis=-1).astype(f32)

    o_a = jnp.dot(heads(oa_ref, N_HEADS_A), wa_ref[...], preferred_element_type=f32)
    merged = gate(0) * o_a
    o_b = jnp.dot(heads(ob_ref, N_HEADS_B), wb_ref[...], preferred_element_type=f32)
    merged = merged + gate(1) * o_b
    o_m = jnp.dot(heads(om_ref, N_HEADS_M), wm_ref[...], preferred_element_type=f32)
    merged = merged + gate(2) * o_m
    hn = h_ref[...] + jnp.dot(merged.astype(bf16), wo_ref[...], preferred_element_type=f32)
    hn_ref[...] = hn

    ms = jnp.mean(hn * hn, axis=-1, keepdims=True)
    xn = ((hn * lax.rsqrt(ms + EPS)) * gn_ref[...]).astype(bf16)
    logits = jnp.dot(xn, wr_ref[...], preferred_element_type=f32) + br_ref[...]
    lt_ref[...] = logits.T[:N_EXPERTS, :]
    _store_row_tiles(xp_ref, _pack_halves(xn))


def _merge(h, proj, oa, ob, om, wa, wb, wm, wo, gn, wr, br, layer, *, tm):
    t, d = h.shape
    ng = 3 * d // LANE
    return pl.pallas_call(
        functools.partial(_merge_body, d=d),
        grid=(t // tm,),
        in_specs=[
            pl.BlockSpec((tm, d), lambda i: (i, 0)),
            pl.BlockSpec((ng, tm, LANE), lambda i: (0, i, 0)),
            pl.BlockSpec((N_HEADS_A, tm, LANE), lambda i: (0, i, 0)),
            pl.BlockSpec((N_HEADS_B, tm, LANE), lambda i: (0, i, 0)),
            pl.BlockSpec((N_HEADS_M, tm, LANE), lambda i: (0, i, 0)),
            _layer_spec(wa.shape, layer), _layer_spec(wb.shape, layer), _layer_spec(wm.shape, layer),
            _layer_spec(wo.shape, layer), _layer_spec(gn.shape, layer),
            _const_spec(wr.shape), _const_spec(br.shape),
        ],
        out_specs=[
            pl.BlockSpec((tm, d), lambda i: (i, 0)),
            pl.BlockSpec((tm * (d // 2 // LANE), LANE), lambda i: (i, 0)),
            pl.BlockSpec((N_EXPERTS, tm), lambda i: (0, i)),
        ],
        out_shape=[
            jax.ShapeDtypeStruct((t, d), f32),
            jax.ShapeDtypeStruct((t * (d // 2 // LANE), LANE), jnp.uint32),
            jax.ShapeDtypeStruct((N_EXPERTS, t), f32),
        ],
        compiler_params=_cparams(("parallel",), 56),
        name="merge_out_router",
    )(h, proj, oa, ob, om, wa, wb, wm, wo, gn, wr, br)


def _route_body(lt_ref, idx_ref, wt_ref, be_ref, nu_ref, run_ref, tot_ref, *, tn, nbp):
    phase = pl.program_id(0)
    i = pl.program_id(1)
    last = pl.num_programs(1) - 1

    @pl.when(i == 0)
    def _():
        run_ref[...] = jnp.zeros_like(run_ref)

    x = lt_ref[...]
    ex = jnp.exp(x - x.max(axis=0, keepdims=True))
    sc = ex / ex.sum(axis=0, keepdims=True)

    def top2(rows):
        m1 = functools.reduce(jnp.maximum, rows)
        i1 = jnp.full_like(m1, float(len(rows) - 1))
        for j in range(len(rows) - 2, -1, -1):
            i1 = jnp.where(rows[j] == m1, float(j), i1)
        rest = [jnp.where(i1 == float(j), -1.0, r) for j, r in enumerate(rows)]
        m2 = functools.reduce(jnp.maximum, rest)
        i2 = jnp.full_like(m2, float(len(rows) - 1))
        for j in range(len(rows) - 2, -1, -1):
            i2 = jnp.where(rest[j] == m2, float(j), i2)
        return m1, i1, m2, i2

    groups = [top2([sc[g * E_PER_GROUP + j:g * E_PER_GROUP + j + 1, :] for j in range(E_PER_GROUP)])
              for g in range(N_GROUPS)]
    gs = [g[0] + g[2] for g in groups]
    best = functools.reduce(jnp.maximum, gs)
    sel = [groups[N_GROUPS - 1][k] for k in range(4)]
    gi = jnp.full_like(best, float(N_GROUPS - 1))
    for g in range(N_GROUPS - 2, -1, -1):
        hit = gs[g] == best
        sel = [jnp.where(hit, groups[g][k], sel[k]) for k in range(4)]
        gi = jnp.where(hit, float(g), gi)
    m1, i1, m2, i2 = sel
    e0 = gi * E_PER_GROUP + i1
    e1 = gi * E_PER_GROUP + i2
    wsum = m1 + m2
    w0 = m1 / wsum
    w1 = m2 / wsum

    erow = lax.broadcasted_iota(jnp.int32, (N_EXPERTS, 1), 0).astype(f32)
    oh0 = (erow == e0).astype(f32)
    oh1 = (erow == e1).astype(f32)
    sel_mask = oh0 + oh1
    r = lax.broadcasted_iota(jnp.int32, (tn, tn), 0)
    c = lax.broadcasted_iota(jnp.int32, (tn, tn), 1)
    before = (r < c).astype(bf16)
    cnt = jnp.dot(sel_mask.astype(bf16), before, preferred_element_type=f32) + run_ref[:, 0:1]
    run_ref[...] = run_ref[...] + sel_mask.sum(axis=1, keepdims=True)

    @pl.when(jnp.logical_and(phase == 0, i == last))
    def _():
        tot_ref[...] = run_ref[...]

    @pl.when(phase == 1)
    def _():
        tot = tot_ref[:, 0:1]
        padded = jnp.floor((tot + (MOE_BLOCK - 1)) / MOE_BLOCK) * MOE_BLOCK
        start = jnp.zeros_like(padded)
        for e in range(N_EXPERTS - 1):
            start = start + jnp.where(erow > float(e), padded[e:e + 1, :], 0.0)
        slot = start + cnt
        d0 = (oh0 * slot).sum(axis=0, keepdims=True)
        d1 = (oh1 * slot).sum(axis=0, keepdims=True)
        r8 = lax.broadcasted_iota(jnp.int32, (8, 1), 0)
        rows = jnp.where(r8 == 0, e0, jnp.where(r8 == 1, e1, jnp.where(r8 == 2, d0, jnp.where(r8 == 3, d1, 0.0))))
        idx_ref[...] = rows.astype(jnp.int32)
        rl = lax.broadcasted_iota(jnp.int32, (LANE, 1), 0)
        wt_ref[...] = jnp.where(rl == 0, w0, jnp.where(rl == 1, w1, 0.0)).T
        end = start + padded
        blk = lax.broadcasted_iota(jnp.int32, (1, nbp), 1).astype(f32) * MOE_BLOCK
        be = (end <= blk).astype(f32).sum(axis=0, keepdims=True)
        be_ref[...] = jnp.minimum(be, N_EXPERTS - 1.0).astype(jnp.int32)
        nu_ref[...] = jnp.broadcast_to(end[N_EXPERTS - 1:, :] / MOE_BLOCK, (1, LANE)).astype(jnp.int32)


def _route(lt, *, tn, nbp):
    t = lt.shape[1]
    return pl.pallas_call(
        functools.partial(_route_body, tn=tn, nbp=nbp),
        grid=(2, t // tn),
        in_specs=[pl.BlockSpec((N_EXPERTS, tn), lambda p, i: (0, i))],
        out_specs=[
            pl.BlockSpec((8, tn), lambda p, i: (0, i * p)),
            pl.BlockSpec((tn, LANE), lambda p, i: (i * p, 0)),
            pl.BlockSpec((1, nbp), lambda p, i: (0, 0)),
            pl.BlockSpec((1, LANE), lambda p, i: (0, 0)),
        ],
        out_shape=[
            jax.ShapeDtypeStruct((8, t), jnp.int32),
            jax.ShapeDtypeStruct((t, LANE), f32),
            jax.ShapeDtypeStruct((1, nbp), jnp.int32),
            jax.ShapeDtypeStruct((1, LANE), jnp.int32),
        ],
        scratch_shapes=[pltpu.VMEM((N_EXPERTS, LANE), f32), pltpu.VMEM((N_EXPERTS, LANE), f32)],
        compiler_params=_cparams(("arbitrary", "arbitrary"), 32),
        name="route",
    )(lt)


def _dispatch_body(dest_ref, x_ref, prev_ref, xb_ref, sem, *, tc, t, rt):
    del prev_ref
    base = pl.program_id(0) * tc

    def copy(r, d):
        return pltpu.make_async_copy(x_ref.at[pl.ds(pl.multiple_of(r * rt, rt), rt), :],
                                     xb_ref.at[pl.ds(pl.multiple_of(d * rt, rt), rt), :], sem)

    def issue(g, carry):
        for u in range(SUBLANES):
            r = g * SUBLANES + u
            copy(r, dest_ref[base + r]).start(priority=0)
            copy(r, dest_ref[t + base + r]).start(priority=1)
        return carry

    lax.fori_loop(0, tc // SUBLANES, issue, 0)

    def drain(g, carry):
        for _ in range(2 * SUBLANES):
            copy(0, 0).wait()
        return carry

    lax.fori_loop(0, tc // SUBLANES, drain, 0)


def _dispatch(dest, xp, slots, *, tc, rt):
    t = xp.shape[0] // rt
    grid_spec = pltpu.PrefetchScalarGridSpec(
        num_scalar_prefetch=1,
        grid=(t // tc,),
        in_specs=[
            pl.BlockSpec((tc * rt, LANE), lambda i, dest: (i, 0)),
            pl.BlockSpec(memory_space=pl.ANY),
        ],
        out_specs=pl.BlockSpec(memory_space=pl.ANY),
        scratch_shapes=[pltpu.SemaphoreType.DMA],
    )
    return pl.pallas_call(
        functools.partial(_dispatch_body, tc=tc, t=t, rt=rt),
        grid_spec=grid_spec,
        out_shape=jax.ShapeDtypeStruct(slots.shape, slots.dtype),
        input_output_aliases={2: 0},
        compiler_params=_cparams(("arbitrary",), 32),
        name="dispatch",
    )(dest, xp, slots)


def _ffn_body(be_ref, nu_ref, x_ref, w1_ref, w3_ref, w2_ref, y_ref, *, dh):
    del be_ref
    b = pl.program_id(0)

    @pl.when(b < nu_ref[0])
    def _():
        tiles = [_unpack_halves(u) for u in _load_row_tiles(x_ref, MOE_BLOCK)]
        lo = jnp.concatenate([p[0] for p in tiles], axis=-1).astype(bf16)
        hi = jnp.concatenate([p[1] for p in tiles], axis=-1).astype(bf16)
        h1 = (jnp.dot(lo, w1_ref[:dh, :], preferred_element_type=f32)
              + jnp.dot(hi, w1_ref[dh:, :], preferred_element_type=f32))
        h3 = (jnp.dot(lo, w3_ref[:dh, :], preferred_element_type=f32)
              + jnp.dot(hi, w3_ref[dh:, :], preferred_element_type=f32))
        hid = (h1 / (1.0 + jnp.exp(-h1))) * h3
        y = jnp.dot(hid.astype(bf16), w2_ref[...], preferred_element_type=f32)
        _store_row_tiles(y_ref, _pack_halves(y))

    @pl.when(b >= nu_ref[0])
    def _():
        y_ref[...] = jnp.zeros_like(y_ref)


def _ffn(block_e, n_used, xb, w1, w3, w2, layer):
    _, _, d, f = w1.shape
    dh = d // 2
    rt = dh // LANE
    nb = xb.shape[0] // (MOE_BLOCK * rt)
    grid_spec = pltpu.PrefetchScalarGridSpec(
        num_scalar_prefetch=2,
        grid=(nb,),
        in_specs=[
            pl.BlockSpec((MOE_BLOCK * rt, LANE), lambda b, be, nu: (b, 0)),
            pl.BlockSpec((None, None, d, f), lambda b, be, nu: (layer, be[b], 0, 0)),
            pl.BlockSpec((None, None, d, f), lambda b, be, nu: (layer, be[b], 0, 0)),
            pl.BlockSpec((None, None, f, d), lambda b, be, nu: (layer, be[b], 0, 0)),
        ],
        out_specs=pl.BlockSpec((MOE_BLOCK * rt, LANE), lambda b, be, nu: (b, 0)),
    )
    return pl.pallas_call(
        functools.partial(_ffn_body, dh=dh),
        grid_spec=grid_spec,
        out_shape=jax.ShapeDtypeStruct(xb.shape, jnp.uint32),
        compiler_params=_cparams(("arbitrary",), 56),
        name="expert_ffn",
    )(block_e, n_used, xb, w1, w3, w2)


def _combine_body(dest_ref, h_ref, wt_ref, g_ref, y_ref, o_ref, buf0, buf1, sem, *, tc, t, rt, final):
    i = pl.program_id(0)
    cur = i % 2

    def copy(d, buf, b, r):
        return pltpu.make_async_copy(y_ref.at[pl.ds(pl.multiple_of(d * rt, rt), rt), :],
                                     buf.at[b, pl.ds(pl.multiple_of(r * rt, rt), rt), :], sem.at[b])

    def gather(step, b):
        base = step * tc

        def issue(g, carry):
            for u in range(SUBLANES):
                r = g * SUBLANES + u
                copy(dest_ref[base + r], buf0, b, r).start(priority=0)
                copy(dest_ref[t + base + r], buf1, b, r).start(priority=1)
            return carry

        lax.fori_loop(0, tc // SUBLANES, issue, 0)

    @pl.when(i == 0)
    def _():
        gather(0, 0)

    @pl.when(i + 1 < pl.num_programs(0))
    def _():
        gather(i + 1, 1 - cur)

    def drain(g, carry):
        for _ in range(SUBLANES):
            copy(0, buf0, cur, 0).wait()
            copy(0, buf1, cur, 0).wait()
        return carry

    lax.fori_loop(0, tc // SUBLANES, drain, 0)
    w = wt_ref[...]
    w0, w1 = w[:, 0:1], w[:, 1:2]
    dh = rt * LANE
    ss = jnp.zeros((tc, 1), f32)
    for k, (u0, u1) in enumerate(zip(_load_row_tiles(buf0.at[cur], tc), _load_row_tiles(buf1.at[cur], tc))):
        lo0, hi0 = _unpack_halves(u0)
        lo1, hi1 = _unpack_halves(u1)
        for cols, y0, y1 in ((slice(k * LANE, (k + 1) * LANE), lo0, lo1),
                             (slice(dh + k * LANE, dh + (k + 1) * LANE), hi0, hi1)):
            out = h_ref[:, cols] + (w0 * y0 + w1 * y1)
            o_ref[:, cols] = out
            if final:
                ss = ss + (out * out).sum(axis=-1, keepdims=True)
    if final:
        inv = lax.rsqrt(ss / (2 * dh) + EPS)
        o_ref[...] = (o_ref[...] * inv) * g_ref[...]


def _combine(dest, h, wt, g, y, *, tc, rt, final):
    t, d = h.shape
    grid_spec = pltpu.PrefetchScalarGridSpec(
        num_scalar_prefetch=1,
        grid=(t // tc,),
        in_specs=[
            pl.BlockSpec((tc, d), lambda i, dest: (i, 0)),
            pl.BlockSpec((tc, LANE), lambda i, dest: (i, 0)),
            pl.BlockSpec((1, d), lambda i, dest: (0, 0)),
            pl.BlockSpec(memory_space=pl.ANY),
        ],
        out_specs=pl.BlockSpec((tc, d), lambda i, dest: (i, 0)),
        scratch_shapes=[pltpu.VMEM((2, tc * rt, LANE), jnp.uint32), pltpu.VMEM((2, tc * rt, LANE), jnp.uint32),
                        pltpu.SemaphoreType.DMA((2,))],
    )
    return pl.pallas_call(
        functools.partial(_combine_body, tc=tc, t=t, rt=rt, final=final),
        grid_spec=grid_spec,
        out_shape=jax.ShapeDtypeStruct((t, d), f32),
        compiler_params=_cparams(("arbitrary",), 48),
        name="combine",
    )(dest, h, wt, g, y)


def _rel_bias_tables(rel_bias):
    depth, heads, _ = rel_bias.shape
    band0 = (BAND_BLOCKS - 1) * QA_BLOCK
    ncol = (2 * BAND_BLOCKS - 1) * QA_BLOCK
    n = ncol + QA_BLOCK
    rb = rel_bias.astype(f32)
    n_lo = band0 - REL_CLIP + QA_BLOCK - 1
    w = jnp.concatenate([
        jnp.broadcast_to(rb[..., :1], (depth, heads, n_lo)), rb,
        jnp.broadcast_to(rb[..., -1:], (depth, heads, n + 1 - n_lo - rb.shape[-1]))], axis=-1)
    skew = jnp.broadcast_to(w[:, :, None, :], (depth, heads, QA_BLOCK, n + 1))
    skew = skew.reshape(depth, heads, QA_BLOCK * (n + 1))[..., :QA_BLOCK * n]
    tab = skew.reshape(depth, heads, QA_BLOCK, n)[..., QA_BLOCK - 1:QA_BLOCK - 1 + ncol]
    row = jnp.arange(QA_BLOCK)[:, None]
    u = jnp.arange(ncol)[None, :]
    kc = u // CHUNK
    qc = row // CHUNK + BAND_CHUNKS
    valid = (kc >= qc - BAND_CHUNKS) & (kc <= qc)
    tab = jnp.where(valid, tab * LOG2E, NEG)
    return tab.reshape(depth, heads, QA_BLOCK, 2 * BAND_BLOCKS - 1, QA_BLOCK).transpose(0, 1, 3, 2, 4)


def _pad_lanes(w):
    pad = [(0, 0)] * (w.ndim - 1) + [(0, LANE - w.shape[-1])]
    return jnp.pad(w, pad)


def kernel(x, mem, norm_mix, w_in, b_forget, rel_bias, norm_mem, w_mem_kv, w_br_a, w_br_b, w_br_m,
           w_out, norm_ffn, w_router, b_router, w1, w3, w2, norm_final):
    batch, seq, d = x.shape
    n_mem = mem.shape[1]
    depth = w_in.shape[0]
    t = batch * seq
    ng = 3 * d // LANE
    assert ng % N_HEADS_A == 0 and (ng + N_QKV_BLOCKS) % N_HEADS_M == 0
    assert seq % 512 == 0 and n_mem % LANE == 0

    tn_proj = min(1024, math.gcd(3 * d, OFF_FB + W_M))
    tm_proj = min(1024, t)
    tq_b = 512
    tc = min(512, t)
    n_slots = (-(-(t * 2) // MOE_BLOCK) + N_EXPERTS) * MOE_BLOCK
    nb = n_slots // MOE_BLOCK
    nbp = -(-nb // LANE) * LANE

    w_main, w_f = _prep_w_in(w_in.astype(bf16), tk=min(256, d))
    g_mix = norm_mix.reshape(depth, 1, d)
    g_mem = norm_mem.reshape(depth, 1, d)
    g_ffn = norm_ffn.reshape(depth, 1, d)
    g_final = norm_final.reshape(1, d)
    bias_f = _pad_lanes(b_forget.astype(f32)).reshape(depth, 1, LANE)
    w_kv = w_mem_kv.astype(bf16)
    zeros_f = jnp.zeros((depth, d, LANE), bf16)
    tables = _rel_bias_tables(rel_bias)
    wa, wb, wm, wo = (w.astype(bf16) for w in (w_br_a, w_br_b, w_br_m, w_out))
    wr = _pad_lanes(w_router).astype(bf16)
    br = _pad_lanes(b_router.reshape(1, -1)).astype(f32)
    w1b, w3b, w2b = (w.astype(bf16) for w in (w1, w3, w2))

    h = x.reshape(t, d)
    memf = mem.reshape(batch * n_mem, d)
    rt = d // 2 // LANE
    xb = jnp.zeros((n_slots * rt, LANE), jnp.uint32)

    for l in range(depth):
        proj, flog = _norm_proj(h, g_mix, w_main, w_f, l, n_sig_cols=3 * d, tm=tm_proj, tn=tn_proj)
        qaug, kp = _forget_cumsum(flog, bias_f, proj, l, batch=batch, seq=seq, tb=tq_b, ng=ng)
        kv, _ = _norm_proj(memf, g_mem, w_kv, zeros_f, l, n_sig_cols=0,
                           tm=min(1024, batch * n_mem), tn=W_M)

        oa = _attn_a(proj, tables, l, batch=batch, seq=seq, ng=ng)
        ob = _attn_b(proj, qaug, kp, batch=batch, seq=seq, ng=ng, tq=tq_b)
        om = _attn_m(proj, kv, batch=batch, seq=seq, n_mem=n_mem, ng=ng, tq=512)

        h, xp, lt = _merge(h, proj, oa, ob, om, wa, wb, wm, wo, g_ffn, wr, br, l, tm=256)

        idx, wt, block_e, n_used = _route(lt, tn=512, nbp=nbp)
        dest = idx[2:4].reshape(2 * t)
        xb = _dispatch(dest, xp, xb, tc=tc, rt=rt)
        yb = _ffn(block_e[0, :nb], n_used[0, :1], xb, w1b, w3b, w2b, l)
        h = _combine(dest, h, wt, g_final, yb, tc=tc, rt=rt, final=(l == depth - 1))

    return h.reshape(batch, seq, d)
```

```python
import functools
import math

import jax
import jax.numpy as jnp
from jax import lax
from jax.experimental import pallas as pl
from jax.experimental.pallas import tpu as pltpu

CHUNK = 64
HEAD_DIM = 128
N_HEADS_A = 6
N_HEADS_B = 6
N_HEADS_M = 4
BAND_CHUNKS = 8
REL_CLIP = 128
N_EXPERTS = 16
N_GROUPS = 4
E_PER_GROUP = 4
MOE_BLOCK = 512
EPS = 1e-6
SCALE = HEAD_DIM ** -0.5
LOG2E = 1.4426950408889634

LANE = 128
SUBLANES = 8
BF16_ROWS = 16
NEG = -1e30
QA_BLOCK = 4 * CHUNK
BAND_BLOCKS = BAND_CHUNKS * CHUNK // QA_BLOCK + 1
N_QKV_BLOCKS = 3 * N_HEADS_A + 3 * N_HEADS_B
W_A = N_HEADS_A * HEAD_DIM
W_B = N_HEADS_B * HEAD_DIM
W_M = N_HEADS_M * HEAD_DIM
OFF_FB = 3 * W_A + 3 * W_B
OFF_QM = OFF_FB + N_HEADS_B
OFF_G = OFF_QM + W_M

f32 = jnp.float32
bf16 = jnp.bfloat16


def _cparams(sem, vmem_mb):
    return pltpu.CompilerParams(dimension_semantics=sem, vmem_limit_bytes=vmem_mb * 1024 * 1024)


def _layer_spec(shape, layer):
    nd = len(shape) - 1
    return pl.BlockSpec((None,) + tuple(shape[1:]), lambda *_: (layer,) + (0,) * nd,
                        pipeline_mode=pl.Buffered(1))


def _pack_halves(x):
    n = x.shape[1] // 2
    xf = x.astype(bf16).astype(f32)
    lo = pltpu.bitcast(xf[:, :n], jnp.uint32)
    hi = pltpu.bitcast(xf[:, n:], jnp.uint32)
    return lax.shift_right_logical(lo, jnp.uint32(16)) | (hi & jnp.uint32(0xFFFF0000))


def _unpack_halves(u):
    lo = pltpu.bitcast(lax.shift_left(u, jnp.uint32(16)), f32)
    hi = pltpu.bitcast(u & jnp.uint32(0xFFFF0000), f32)
    return lo, hi


def _store_row_tiles(ref, packed):
    m, w = packed.shape
    r = w // LANE
    for k in range(r):
        ref[pl.ds(k, m, stride=r), :] = packed[:, k * LANE:(k + 1) * LANE]


def _load_row_tiles(ref, m):
    r = ref.shape[0] // m
    return [ref[pl.ds(k, m, stride=r), :] for k in range(r)]


def _const_spec(shape):
    nd = len(shape)
    return pl.BlockSpec(shape, lambda *_: (0,) * nd, pipeline_mode=pl.Buffered(1))


def _prep_body(w_ref, o_ref, f_ref, *, d):
    o_ref[:, :3 * d] = w_ref[:, OFF_G:OFF_G + 3 * d].astype(bf16)
    o_ref[:, 3 * d:3 * d + OFF_FB] = w_ref[:, :OFF_FB].astype(bf16)
    o_ref[:, 3 * d + OFF_FB:] = w_ref[:, OFF_QM:OFF_G].astype(bf16)
    lane = lax.broadcasted_iota(jnp.int32, (1, LANE), 1)
    f_ref[...] = jnp.where(lane < N_HEADS_B, w_ref[:, OFF_FB:OFF_FB + LANE], 0.0).astype(bf16)


def _prep_w_in(w_in, *, tk):
    depth, d, n_in = w_in.shape
    nm = 3 * d + OFF_FB + W_M
    return pl.pallas_call(
        functools.partial(_prep_body, d=d),
        grid=(depth, d // tk),
        in_specs=[pl.BlockSpec((None, tk, n_in), lambda l, k: (l, k, 0))],
        out_specs=[
            pl.BlockSpec((None, tk, nm), lambda l, k: (l, k, 0)),
            pl.BlockSpec((None, tk, LANE), lambda l, k: (l, k, 0)),
        ],
        out_shape=[
            jax.ShapeDtypeStruct((depth, d, nm), bf16),
            jax.ShapeDtypeStruct((depth, d, LANE), bf16),
        ],
        compiler_params=_cparams(("parallel", "parallel"), 48),
        name="prep_w_in",
    )(w_in)


def _proj_body(x_ref, g_ref, w_ref, wf_ref, o_ref, f_ref, xn_ref, *, n_sig, ncb):
    j = pl.program_id(1)

    @pl.when(j == 0)
    def _():
        x = x_ref[...]
        ms = jnp.mean(x * x, axis=-1, keepdims=True)
        xn = ((x * lax.rsqrt(ms + EPS)) * g_ref[...]).astype(bf16)
        xn_ref[...] = xn
        f_ref[...] = jnp.dot(xn, wf_ref[...], preferred_element_type=f32)

    acc = jnp.dot(xn_ref[...], w_ref[...], preferred_element_type=f32)
    for c in range(ncb):
        a = acc[:, c * LANE:(c + 1) * LANE]
        if n_sig > 0:
            a = jnp.where(j < n_sig, 1.0 / (1.0 + jnp.exp(-a)), a)
        o_ref[c] = a.astype(bf16)


def _norm_proj(x, g, w, wf, layer, *, n_sig_cols, tm, tn):
    m, d = x.shape
    n = w.shape[2]
    ncb = tn // LANE
    return pl.pallas_call(
        functools.partial(_proj_body, n_sig=n_sig_cols // tn, ncb=ncb),
        grid=(m // tm, n // tn),
        in_specs=[
            pl.BlockSpec((tm, d), lambda i, j: (i, 0)),
            pl.BlockSpec((None, 1, d), lambda i, j: (layer, 0, 0)),
            pl.BlockSpec((None, d, tn), lambda i, j: (layer, 0, j)),
            pl.BlockSpec((None, d, LANE), lambda i, j: (layer, 0, 0)),
        ],
        out_specs=[
            pl.BlockSpec((ncb, tm, LANE), lambda i, j: (j, i, 0)),
            pl.BlockSpec((tm, LANE), lambda i, j: (i, 0)),
        ],
        out_shape=[
            jax.ShapeDtypeStruct((n // LANE, m, LANE), bf16),
            jax.ShapeDtypeStruct((m, LANE), f32),
        ],
        scratch_shapes=[pltpu.VMEM((tm, d), bf16)],
        compiler_params=_cparams(("parallel", "arbitrary"), 56),
        name="norm_proj",
    )(x, g, w, wf)


def _split3(x):
    hi = x.astype(bf16)
    r1 = x - hi.astype(f32)
    mid = r1.astype(bf16)
    lo = (r1 - mid.astype(f32)).astype(bf16)
    return hi, mid, lo


def _fcum_body(fl_ref, b_ref, k_ref, qa_ref, kp_ref, carry_ref, *, tb):
    @pl.when(pl.program_id(1) == 0)
    def _():
        carry_ref[...] = jnp.zeros_like(carry_ref)

    x = fl_ref[...] + b_ref[...]
    lf = jnp.minimum(x, 0.0) - jnp.log1p(jnp.exp(-jnp.abs(x)))
    hi, mid, lo = _split3(lf)
    row = lax.broadcasted_iota(jnp.int32, (tb, tb), 0)
    col = lax.broadcasted_iota(jnp.int32, (tb, tb), 1)
    tri = (col <= row).astype(bf16)
    c = (jnp.dot(tri, hi, preferred_element_type=f32)
         + jnp.dot(tri, mid, preferred_element_type=f32)
         + jnp.dot(tri, lo, preferred_element_type=f32))
    c = c + carry_ref[...]
    carry_ref[...] = c[tb - 1:tb, :]

    lane = lax.broadcasted_iota(jnp.int32, (1, LANE), 1)
    for h in range(N_HEADS_B):
        fh, fm, fl = (p.astype(f32) for p in _split3(c[:, h:h + 1] * LOG2E))
        pieces = jnp.where(lane % 3 == 0, fh, jnp.where(lane % 3 == 1, fm, fl))
        qa = jnp.where(lane < 3, pieces, jnp.where(lane < 6, 1.0, 0.0))
        ka = jnp.where(lane < 3, 1.0, jnp.where(lane < 6, -pieces, 0.0))
        qa_ref[h] = qa.astype(bf16)
        kp_ref[h, :, :HEAD_DIM] = k_ref[h]
        kp_ref[h, :, HEAD_DIM:] = ka.astype(bf16)


def _forget_cumsum(flog, bias, proj, layer, *, batch, seq, tb, ng):
    nsb = seq // tb
    t = batch * seq
    kbase = ng // N_HEADS_B + 4
    return pl.pallas_call(
        functools.partial(_fcum_body, tb=tb),
        grid=(batch, nsb),
        in_specs=[
            pl.BlockSpec((tb, LANE), lambda b, s: (b * nsb + s, 0)),
            pl.BlockSpec((None, 1, LANE), lambda b, s: (layer, 0, 0)),
            pl.BlockSpec((N_HEADS_B, tb, LANE), lambda b, s: (kbase, b * nsb + s, 0)),
        ],
        out_specs=[
            pl.BlockSpec((N_HEADS_B, tb, LANE), lambda b, s: (0, b * nsb + s, 0)),
            pl.BlockSpec((N_HEADS_B, tb, 2 * HEAD_DIM), lambda b, s: (0, b * nsb + s, 0)),
        ],
        out_shape=[
            jax.ShapeDtypeStruct((N_HEADS_B, t, LANE), bf16),
            jax.ShapeDtypeStruct((N_HEADS_B, t, 2 * HEAD_DIM), bf16),
        ],
        scratch_shapes=[pltpu.VMEM((1, LANE), f32)],
        compiler_params=_cparams(("parallel", "arbitrary"), 32),
        name="forget_cumsum",
    )(flog, bias, proj)


def _attn_a_body(q_ref, k_ref, v_ref, tb_ref, o_ref):
    i = pl.program_id(1)
    first = jnp.maximum(i - (BAND_BLOCKS - 1), 0)
    cb0 = jnp.maximum((BAND_BLOCKS - 1) - i, 0)
    for h in range(N_HEADS_A):
        q = (q_ref[h].astype(f32) * (SCALE * LOG2E)).astype(bf16)
        s_blocks = []
        for c in range(BAND_BLOCKS):
            ks = pl.multiple_of((first + c) * QA_BLOCK, QA_BLOCK)
            kb = k_ref[h, pl.ds(ks, QA_BLOCK), :]
            s = lax.dot_general(q, kb, (((1,), (1,)), ((), ())), preferred_element_type=f32)
            s_blocks.append(s + tb_ref[h, cb0 + c])
        m = s_blocks[0].max(axis=-1, keepdims=True)
        for s in s_blocks[1:]:
            m = jnp.maximum(m, s.max(axis=-1, keepdims=True))
        l = jnp.zeros_like(m)
        acc = jnp.zeros((QA_BLOCK, HEAD_DIM), f32)
        for c in range(BAND_BLOCKS):
            p = jnp.exp2(s_blocks[c] - m)
            l = l + p.sum(axis=-1, keepdims=True)
            ks = pl.multiple_of((first + c) * QA_BLOCK, QA_BLOCK)
            vb = v_ref[h, pl.ds(ks, QA_BLOCK), :]
            acc = acc + jnp.dot(p.astype(bf16), vb, preferred_element_type=f32)
        o_ref[h] = (acc / l).astype(bf16)


def _attn_a(proj, tables, layer, *, batch, seq, ng):
    nq = seq // QA_BLOCK
    base = ng // N_HEADS_A
    t = batch * seq
    return pl.pallas_call(
        _attn_a_body,
        grid=(batch, nq),
        in_specs=[
            pl.BlockSpec((N_HEADS_A, QA_BLOCK, LANE), lambda b, i: (base, b * nq + i, 0)),
            pl.BlockSpec((N_HEADS_A, seq, LANE), lambda b, i: (base + 1, b, 0)),
            pl.BlockSpec((N_HEADS_A, seq, LANE), lambda b, i: (base + 2, b, 0)),
            _layer_spec(tables.shape, layer),
        ],
        out_specs=pl.BlockSpec((N_HEADS_A, QA_BLOCK, LANE), lambda b, i: (0, b * nq + i, 0)),
        out_shape=jax.ShapeDtypeStruct((N_HEADS_A, t, LANE), bf16),
        compiler_params=_cparams(("parallel", "arbitrary"), 48),
        name="attn_chunk",
    )(proj, proj, proj, tables)


def _attn_b_body(q_ref, v_ref, qa_ref, kp_ref, o_ref, *, tq):
    i = pl.program_id(1)
    row = lax.broadcasted_iota(jnp.int32, (tq, tq), 0)
    col = lax.broadcasted_iota(jnp.int32, (tq, tq), 1)
    causal = col <= row

    def one_head(h, qp, kb, carry, masked):
        m, l, acc = carry
        ks = pl.multiple_of(kb * tq, tq)
        kp = kp_ref[h, pl.ds(ks, tq), :]
        s = lax.dot_general(qp, kp, (((1,), (1,)), ((), ())), preferred_element_type=f32)
        if masked:
            s = jnp.where(causal, s, NEG)
        m_new = jnp.maximum(m, s.max(axis=-1, keepdims=True))
        alpha = jnp.exp2(m - m_new)
        p = jnp.exp2(s - m_new)
        l = alpha * l + p.sum(axis=-1, keepdims=True)
        acc = alpha * acc + jnp.dot(p.astype(bf16), v_ref[h, pl.ds(ks, tq), :],
                                    preferred_element_type=f32)
        return m_new, l, acc

    for h0 in range(0, N_HEADS_B, 6):
        hs = tuple(range(h0, h0 + 6))
        qps = [jnp.concatenate([(q_ref[h].astype(f32) * (SCALE * LOG2E)).astype(bf16), qa_ref[h]], axis=-1)
               for h in hs]

        def step(kb, carry, masked):
            return tuple(one_head(h, qp, kb, c, masked) for h, qp, c in zip(hs, qps, carry))

        init = (jnp.full((tq, 1), NEG, f32), jnp.zeros((tq, 1), f32), jnp.zeros((tq, HEAD_DIM), f32))
        carry = lax.fori_loop(0, i, lambda kb, c: step(kb, c, False), (init,) * len(hs))
        for h, (m, l, acc) in zip(hs, step(i, carry, True)):
            o_ref[h] = (acc / l).astype(bf16)


def _attn_b(proj, qaug, kp, *, batch, seq, ng, tq):
    nq = seq // tq
    base = ng // N_HEADS_B + 3
    t = batch * seq
    return pl.pallas_call(
        functools.partial(_attn_b_body, tq=tq),
        grid=(batch, nq),
        in_specs=[
            pl.BlockSpec((N_HEADS_B, tq, LANE), lambda b, i: (base, b * nq + i, 0)),
            pl.BlockSpec((N_HEADS_B, seq, LANE), lambda b, i: (base + 2, b, 0)),
            pl.BlockSpec((N_HEADS_B, tq, LANE), lambda b, i: (0, b * nq + i, 0)),
            pl.BlockSpec((N_HEADS_B, seq, 2 * HEAD_DIM), lambda b, i: (0, b, 0)),
        ],
        out_specs=pl.BlockSpec((N_HEADS_B, tq, LANE), lambda b, i: (0, b * nq + i, 0)),
        out_shape=jax.ShapeDtypeStruct((N_HEADS_B, t, LANE), bf16),
        compiler_params=_cparams(("parallel", "arbitrary"), 56),
        name="attn_forget",
    )(proj, proj, qaug, kp)


def _attn_m_body(q_ref, k_ref, v_ref, o_ref):
    for h in range(N_HEADS_M):
        s = lax.dot_general(q_ref[h], k_ref[h], (((1,), (1,)), ((), ())), preferred_element_type=f32)
        s = s * SCALE
        m = s.max(axis=-1, keepdims=True)
        p = jnp.exp(s - m)
        l = p.sum(axis=-1, keepdims=True)
        acc = jnp.dot(p.astype(bf16), v_ref[h], preferred_element_type=f32)
        o_ref[h] = (acc / l).astype(bf16)


def _attn_m(proj, kv, *, batch, seq, n_mem, ng, tq):
    nq = seq // tq
    base = (ng + N_QKV_BLOCKS) // N_HEADS_M
    t = batch * seq
    return pl.pallas_call(
        _attn_m_body,
        grid=(batch, nq),
        in_specs=[
            pl.BlockSpec((N_HEADS_M, tq, LANE), lambda b, i: (base, b * nq + i, 0)),
            pl.BlockSpec((N_HEADS_M, n_mem, LANE), lambda b, i: (0, b, 0)),
            pl.BlockSpec((N_HEADS_M, n_mem, LANE), lambda b, i: (1, b, 0)),
        ],
        out_specs=pl.BlockSpec((N_HEADS_M, tq, LANE), lambda b, i: (0, b * nq + i, 0)),
        out_shape=jax.ShapeDtypeStruct((N_HEADS_M, t, LANE), bf16),
        compiler_params=_cparams(("parallel", "arbitrary"), 32),
        name="attn_mem",
    )(proj, kv, kv)


def _merge_body(h_ref, g_ref, oa_ref, ob_ref, om_ref, wa_ref, wb_ref, wm_ref, wo_ref,
                gn_ref, wr_ref, br_ref, hn_ref, xp_ref, lt_ref, *, d):
    ndb = d // LANE

    def heads(ref, n):
        return jnp.concatenate([ref[h] for h in range(n)], axis=-1)

    def gate(br):
        return jnp.concatenate([g_ref[br * ndb + c] for c in range(ndb)], axis=-1).astype(f32)

    o_a = jnp.dot(heads(oa_ref, N_HEADS_A), wa_ref[...], preferred_element_type=f32)
    merged = gate(0) * o_a
    o_b = jnp.dot(heads(ob_ref, N_HEADS_B), wb_ref[...], preferred_element_type=f32)
    merged = merged + gate(1) * o_b
    o_m = jnp.dot(heads(om_ref, N_HEADS_M), wm_ref[...], preferred_element_type=f32)
    merged = merged + gate(2) * o_m
    hn = h_ref[...] + jnp.dot(merged.astype(bf16), wo_ref[...], preferred_element_type=f32)
    hn_ref[...] = hn

    ms = jnp.mean(hn * hn, axis=-1, keepdims=True)
    xn = ((hn * lax.rsqrt(ms + EPS)) * gn_ref[...]).astype(bf16)
    logits = jnp.dot(xn, wr_ref[...], preferred_element_type=f32) + br_ref[...]
    lt_ref[...] = logits.T[:N_EXPERTS, :]
    _store_row_tiles(xp_ref, _pack_halves(xn))


def _merge(h, proj, oa, ob, om, wa, wb, wm, wo, gn, wr, br, layer, *, tm):
    t, d = h.shape
    ng = 3 * d // LANE
    return pl.pallas_call(
        functools.partial(_merge_body, d=d),
        grid=(t // tm,),
        in_specs=[
            pl.BlockSpec((tm, d), lambda i: (i, 0)),
            pl.BlockSpec((ng, tm, LANE), lambda i: (0, i, 0)),
            pl.BlockSpec((N_HEADS_A, tm, LANE), lambda i: (0, i, 0)),
            pl.BlockSpec((N_HEADS_B, tm, LANE), lambda i: (0, i, 0)),
            pl.BlockSpec((N_HEADS_M, tm, LANE), lambda i: (0, i, 0)),
            _layer_spec(wa.shape, layer), _layer_spec(wb.shape, layer), _layer_spec(wm.shape, layer),
            _layer_spec(wo.shape, layer), _layer_spec(gn.shape, layer),
            _const_spec(wr.shape), _const_spec(br.shape),
        ],
        out_specs=[
            pl.BlockSpec((tm, d), lambda i: (i, 0)),
            pl.BlockSpec((tm * (d // 2 // LANE), LANE), lambda i: (i, 0)),
            pl.BlockSpec((N_EXPERTS, tm), lambda i: (0, i)),
        ],
        out_shape=[
            jax.ShapeDtypeStruct((t, d), f32),
            jax.ShapeDtypeStruct((t * (d // 2 // LANE), LANE), jnp.uint32),
            jax.ShapeDtypeStruct((N_EXPERTS, t), f32),
        ],
        compiler_params=_cparams(("parallel",), 56),
        name="merge_out_router",
    )(h, proj, oa, ob, om, wa, wb, wm, wo, gn, wr, br)


def _route_body(lt_ref, idx_ref, wt_ref, be_ref, nu_ref, run_ref, tot_ref, *, tn, nbp):
    phase = pl.program_id(0)
    i = pl.program_id(1)
    last = pl.num_programs(1) - 1

    @pl.when(i == 0)
    def _():
        run_ref[...] = jnp.zeros_like(run_ref)

    x = lt_ref[...]
    ex = jnp.exp(x - x.max(axis=0, keepdims=True))
    sc = ex / ex.sum(axis=0, keepdims=True)

    def top2(rows):
        m1 = functools.reduce(jnp.maximum, rows)
        i1 = jnp.full_like(m1, float(len(rows) - 1))
        for j in range(len(rows) - 2, -1, -1):
            i1 = jnp.where(rows[j] == m1, float(j), i1)
        rest = [jnp.where(i1 == float(j), -1.0, r) for j, r in enumerate(rows)]
        m2 = functools.reduce(jnp.maximum, rest)
        i2 = jnp.full_like(m2, float(len(rows) - 1))
        for j in range(len(rows) - 2, -1, -1):
            i2 = jnp.where(rest[j] == m2, float(j), i2)
        return m1, i1, m2, i2

    groups = [top2([sc[g * E_PER_GROUP + j:g * E_PER_GROUP + j + 1, :] for j in range(E_PER_GROUP)])
              for g in range(N_GROUPS)]
    gs = [g[0] + g[2] for g in groups]
    best = functools.reduce(jnp.maximum, gs)
    sel = [groups[N_GROUPS - 1][k] for k in range(4)]
    gi = jnp.full_like(best, float(N_GROUPS - 1))
    for g in range(N_GROUPS - 2, -1, -1):
        hit = gs[g] == best
        sel = [jnp.where(hit, groups[g][k], sel[k]) for k in range(4)]
        gi = jnp.where(hit, float(g), gi)
    m1, i1, m2, i2 = sel
    e0 = gi * E_PER_GROUP + i1
    e1 = gi * E_PER_GROUP + i2
    wsum = m1 + m2
    w0 = m1 / wsum
    w1 = m2 / wsum

    erow = lax.broadcasted_iota(jnp.int32, (N_EXPERTS, 1), 0).astype(f32)
    oh0 = (erow == e0).astype(f32)
    oh1 = (erow == e1).astype(f32)
    sel_mask = oh0 + oh1
    r = lax.broadcasted_iota(jnp.int32, (tn, tn), 0)
    c = lax.broadcasted_iota(jnp.int32, (tn, tn), 1)
    before = (r < c).astype(bf16)
    cnt = jnp.dot(sel_mask.astype(bf16), before, preferred_element_type=f32) + run_ref[:, 0:1]
    run_ref[...] = run_ref[...] + sel_mask.sum(axis=1, keepdims=True)

    @pl.when(jnp.logical_and(phase == 0, i == last))
    def _():
        tot_ref[...] = run_ref[...]

    @pl.when(phase == 1)
    def _():
        tot = tot_ref[:, 0:1]
        padded = jnp.floor((tot + (MOE_BLOCK - 1)) / MOE_BLOCK) * MOE_BLOCK
        start = jnp.zeros_like(padded)
        for e in range(N_EXPERTS - 1):
            start = start + jnp.where(erow > float(e), padded[e:e + 1, :], 0.0)
        slot = start + cnt
        d0 = (oh0 * slot).sum(axis=0, keepdims=True)
        d1 = (oh1 * slot).sum(axis=0, keepdims=True)
        r8 = lax.broadcasted_iota(jnp.int32, (8, 1), 0)
        rows = jnp.where(r8 == 0, e0, jnp.where(r8 == 1, e1, jnp.where(r8 == 2, d0, jnp.where(r8 == 3, d1, 0.0))))
        idx_ref[...] = rows.astype(jnp.int32)
        rl = lax.broadcasted_iota(jnp.int32, (LANE, 1), 0)
        wt_ref[...] = jnp.where(rl == 0, w0, jnp.where(rl == 1, w1, 0.0)).T
        end = start + padded
        blk = lax.broadcasted_iota(jnp.int32, (1, nbp), 1).astype(f32) * MOE_BLOCK
        be = (end <= blk).astype(f32).sum(axis=0, keepdims=True)
        be_ref[...] = jnp.minimum(be, N_EXPERTS - 1.0).astype(jnp.int32)
        nu_ref[...] = jnp.broadcast_to(end[N_EXPERTS - 1:, :] / MOE_BLOCK, (1, LANE)).astype(jnp.int32)


def _route(lt, *, tn, nbp):
    t = lt.shape[1]
    return pl.pallas_call(
        functools.partial(_route_body, tn=tn, nbp=nbp),
        grid=(2, t // tn),
        in_specs=[pl.BlockSpec((N_EXPERTS, tn), lambda p, i: (0, i))],
        out_specs=[
            pl.BlockSpec((8, tn), lambda p, i: (0, i * p)),
            pl.BlockSpec((tn, LANE), lambda p, i: (i * p, 0)),
            pl.BlockSpec((1, nbp), lambda p, i: (0, 0)),
            pl.BlockSpec((1, LANE), lambda p, i: (0, 0)),
        ],
        out_shape=[
            jax.ShapeDtypeStruct((8, t), jnp.int32),
            jax.ShapeDtypeStruct((t, LANE), f32),
            jax.ShapeDtypeStruct((1, nbp), jnp.int32),
            jax.ShapeDtypeStruct((1, LANE), jnp.int32),
        ],
        scratch_shapes=[pltpu.VMEM((N_EXPERTS, LANE), f32), pltpu.VMEM((N_EXPERTS, LANE), f32)],
        compiler_params=_cparams(("arbitrary", "arbitrary"), 32),
        name="route",
    )(lt)


def _dispatch_body(dest_ref, x_ref, prev_ref, xb_ref, sem, *, tc, t, rt):
    del prev_ref
    base = pl.program_id(0) * tc

    def copy(r, d):
        return pltpu.make_async_copy(x_ref.at[pl.ds(pl.multiple_of(r * rt, rt), rt), :],
                                     xb_ref.at[pl.ds(pl.multiple_of(d * rt, rt), rt), :], sem)

    def issue(g, carry):
        for u in range(SUBLANES):
            r = g * SUBLANES + u
            copy(r, dest_ref[base + r]).start(priority=0)
            copy(r, dest_ref[t + base + r]).start(priority=1)
        return carry

    lax.fori_loop(0, tc // SUBLANES, issue, 0)

    def drain(g, carry):
        for _ in range(2 * SUBLANES):
            copy(0, 0).wait()
        return carry

    lax.fori_loop(0, tc // SUBLANES, drain, 0)


def _dispatch(dest, xp, slots, *, tc, rt):
    t = xp.shape[0] // rt
    grid_spec = pltpu.PrefetchScalarGridSpec(
        num_scalar_prefetch=1,
        grid=(t // tc,),
        in_specs=[
            pl.BlockSpec((tc * rt, LANE), lambda i, dest: (i, 0)),
            pl.BlockSpec(memory_space=pl.ANY),
        ],
        out_specs=pl.BlockSpec(memory_space=pl.ANY),
        scratch_shapes=[pltpu.SemaphoreType.DMA],
    )
    return pl.pallas_call(
        functools.partial(_dispatch_body, tc=tc, t=t, rt=rt),
        grid_spec=grid_spec,
        out_shape=jax.ShapeDtypeStruct(slots.shape, slots.dtype),
        input_output_aliases={2: 0},
        compiler_params=_cparams(("arbitrary",), 32),
        name="dispatch",
    )(dest, xp, slots)


def _ffn_body(be_ref, nu_ref, x_ref, w1_ref, w3_ref, w2_ref, y_ref, *, dh):
    del be_ref
    b = pl.program_id(0)

    @pl.when(b < nu_ref[0])
    def _():
        tiles = [_unpack_halves(u) for u in _load_row_tiles(x_ref, MOE_BLOCK)]
        lo = jnp.concatenate([p[0] for p in tiles], axis=-1).astype(bf16)
        hi = jnp.concatenate([p[1] for p in tiles], axis=-1).astype(bf16)
        h1 = (jnp.dot(lo, w1_ref[:dh, :], preferred_element_type=f32)
              + jnp.dot(hi, w1_ref[dh:, :], preferred_element_type=f32))
        h3 = (jnp.dot(lo, w3_ref[:dh, :], preferred_element_type=f32)
              + jnp.dot(hi, w3_ref[dh:, :], preferred_element_type=f32))
        hid = (h1 / (1.0 + jnp.exp(-h1))) * h3
        y = jnp.dot(hid.astype(bf16), w2_ref[...], preferred_element_type=f32)
        _store_row_tiles(y_ref, _pack_halves(y))

    @pl.when(b >= nu_ref[0])
    def _():
        y_ref[...] = jnp.zeros_like(y_ref)


def _ffn(block_e, n_used, xb, w1, w3, w2, layer):
    _, _, d, f = w1.shape
    dh = d // 2
    rt = dh // LANE
    nb = xb.shape[0] // (MOE_BLOCK * rt)
    grid_spec = pltpu.PrefetchScalarGridSpec(
        num_scalar_prefetch=2,
        grid=(nb,),
        in_specs=[
            pl.BlockSpec((MOE_BLOCK * rt, LANE), lambda b, be, nu: (b, 0)),
            pl.BlockSpec((None, None, d, f), lambda b, be, nu: (layer, be[b], 0, 0)),
            pl.BlockSpec((None, None, d, f), lambda b, be, nu: (layer, be[b], 0, 0)),
            pl.BlockSpec((None, None, f, d), lambda b, be, nu: (layer, be[b], 0, 0)),
        ],
        out_specs=pl.BlockSpec((MOE_BLOCK * rt, LANE), lambda b, be, nu: (b, 0)),
    )
    return pl.pallas_call(
        functools.partial(_ffn_body, dh=dh),
        grid_spec=grid_spec,
        out_shape=jax.ShapeDtypeStruct(xb.shape, jnp.uint32),
        compiler_params=_cparams(("arbitrary",), 56),
        name="expert_ffn",
    )(block_e, n_used, xb, w1, w3, w2)


def _combine_body(dest_ref, h_ref, wt_ref, g_ref, y_ref, o_ref, buf0, buf1, sem, *, tc, t, rt, final):
    i = pl.program_id(0)
    cur = i % 2

    def copy(d, buf, b, r):
        return pltpu.make_async_copy(y_ref.at[pl.ds(pl.multiple_of(d * rt, rt), rt), :],
                                     buf.at[b, pl.ds(pl.multiple_of(r * rt, rt), rt), :], sem.at[b])

    def gather(step, b):
        base = step * tc

        def issue(g, carry):
            for u in range(SUBLANES):
                r = g * SUBLANES + u
                copy(dest_ref[base + r], buf0, b, r).start(priority=0)
                copy(dest_ref[t + base + r], buf1, b, r).start(priority=1)
            return carry

        lax.fori_loop(0, tc // SUBLANES, issue, 0)

    @pl.when(i == 0)
    def _():
        gather(0, 0)

    @pl.when(i + 1 < pl.num_programs(0))
    def _():
        gather(i + 1, 1 - cur)

    def drain(g, carry):
        for _ in range(SUBLANES):
            copy(0, buf0, cur, 0).wait()
            copy(0, buf1, cur, 0).wait()
        return carry

    lax.fori_loop(0, tc // SUBLANES, drain, 0)
    w = wt_ref[...]
    w0, w1 = w[:, 0:1], w[:, 1:2]
    dh = rt * LANE
    ss = jnp.zeros((tc, 1), f32)
    for k, (u0, u1) in enumerate(zip(_load_row_tiles(buf0.at[cur], tc), _load_row_tiles(buf1.at[cur], tc))):
        lo0, hi0 = _unpack_halves(u0)
        lo1, hi1 = _unpack_halves(u1)
        for cols, y0, y1 in ((slice(k * LANE, (k + 1) * LANE), lo0, lo1),
                             (slice(dh + k * LANE, dh + (k + 1) * LANE), hi0, hi1)):
            out = h_ref[:, cols] + (w0 * y0 + w1 * y1)
            o_ref[:, cols] = out
            if final:
                ss = ss + (out * out).sum(axis=-1, keepdims=True)
    if final:
        inv = lax.rsqrt(ss / (2 * dh) + EPS)
        o_ref[...] = (o_ref[...] * inv) * g_ref[...]


def _combine(dest, h, wt, g, y, *, tc, rt, final):
    t, d = h.shape
    grid_spec = pltpu.PrefetchScalarGridSpec(
        num_scalar_prefetch=1,
        grid=(t // tc,),
        in_specs=[
            pl.BlockSpec((tc, d), lambda i, dest: (i, 0)),
            pl.BlockSpec((tc, LANE), lambda i, dest: (i, 0)),
            pl.BlockSpec((1, d), lambda i, dest: (0, 0)),
            pl.BlockSpec(memory_space=pl.ANY),
        ],
        out_specs=pl.BlockSpec((tc, d), lambda i, dest: (i, 0)),
        scratch_shapes=[pltpu.VMEM((2, tc * rt, LANE), jnp.uint32), pltpu.VMEM((2, tc * rt, LANE), jnp.uint32),
                        pltpu.SemaphoreType.DMA((2,))],
    )
    return pl.pallas_call(
        functools.partial(_combine_body, tc=tc, t=t, rt=rt, final=final),
        grid_spec=grid_spec,
        out_shape=jax.ShapeDtypeStruct((t, d), f32),
        compiler_params=_cparams(("arbitrary",), 48),
        name="combine",
    )(dest, h, wt, g, y)


def _rel_bias_tables(rel_bias):
    depth, heads, _ = rel_bias.shape
    band0 = (BAND_BLOCKS - 1) * QA_BLOCK
    ncol = (2 * BAND_BLOCKS - 1) * QA_BLOCK
    n = ncol + QA_BLOCK
    rb = rel_bias.astype(f32)
    n_lo = band0 - REL_CLIP + QA_BLOCK - 1
    w = jnp.concatenate([
        jnp.broadcast_to(rb[..., :1], (depth, heads, n_lo)), rb,
        jnp.broadcast_to(rb[..., -1:], (depth, heads, n + 1 - n_lo - rb.shape[-1]))], axis=-1)
    skew = jnp.broadcast_to(w[:, :, None, :], (depth, heads, QA_BLOCK, n + 1))
    skew = skew.reshape(depth, heads, QA_BLOCK * (n + 1))[..., :QA_BLOCK * n]
    tab = skew.reshape(depth, heads, QA_BLOCK, n)[..., QA_BLOCK - 1:QA_BLOCK - 1 + ncol]
    row = jnp.arange(QA_BLOCK)[:, None]
    u = jnp.arange(ncol)[None, :]
    kc = u // CHUNK
    qc = row // CHUNK + BAND_CHUNKS
    valid = (kc >= qc - BAND_CHUNKS) & (kc <= qc)
    tab = jnp.where(valid, tab * LOG2E, NEG)
    return tab.reshape(depth, heads, QA_BLOCK, 2 * BAND_BLOCKS - 1, QA_BLOCK).transpose(0, 1, 3, 2, 4)


def _pad_lanes(w):
    pad = [(0, 0)] * (w.ndim - 1) + [(0, LANE - w.shape[-1])]
    return jnp.pad(w, pad)


def kernel(x, mem, norm_mix, w_in, b_forget, rel_bias, norm_mem, w_mem_kv, w_br_a, w_br_b, w_br_m,
           w_out, norm_ffn, w_router, b_router, w1, w3, w2, norm_final):
    batch, seq, d = x.shape
    n_mem = mem.shape[1]
    depth = w_in.shape[0]
    t = batch * seq
    ng = 3 * d // LANE
    assert ng % N_HEADS_A == 0 and (ng + N_QKV_BLOCKS) % N_HEADS_M == 0
    assert seq % 512 == 0 and n_mem % LANE == 0

    tn_proj = min(1024, math.gcd(3 * d, OFF_FB + W_M))
    tm_proj = min(1024, t)
    tq_b = 512
    tc = min(512, t)
    n_slots = (-(-(t * 2) // MOE_BLOCK) + N_EXPERTS) * MOE_BLOCK
    nb = n_slots // MOE_BLOCK
    nbp = -(-nb // LANE) * LANE

    w_main, w_f = _prep_w_in(w_in.astype(bf16), tk=min(256, d))
    g_mix = norm_mix.reshape(depth, 1, d)
    g_mem = norm_mem.reshape(depth, 1, d)
    g_ffn = norm_ffn.reshape(depth, 1, d)
    g_final = norm_final.reshape(1, d)
    bias_f = _pad_lanes(b_forget.astype(f32)).reshape(depth, 1, LANE)
    w_kv = w_mem_kv.astype(bf16)
    zeros_f = jnp.zeros((depth, d, LANE), bf16)
    tables = _rel_bias_tables(rel_bias)
    wa, wb, wm, wo = (w.astype(bf16) for w in (w_br_a, w_br_b, w_br_m, w_out))
    wr = _pad_lanes(w_router).astype(bf16)
    br = _pad_lanes(b_router.reshape(1, -1)).astype(f32)
    w1b, w3b, w2b = (w.astype(bf16) for w in (w1, w3, w2))

    h = x.reshape(t, d)
    memf = mem.reshape(batch * n_mem, d)
    rt = d // 2 // LANE
    xb = jnp.zeros((n_slots * rt, LANE), jnp.uint32)

    for l in range(depth):
        proj, flog = _norm_proj(h, g_mix, w_main, w_f, l, n_sig_cols=3 * d, tm=tm_proj, tn=tn_proj)
        qaug, kp = _forget_cumsum(flog, bias_f, proj, l, batch=batch, seq=seq, tb=tq_b, ng=ng)
        kv, _ = _norm_proj(memf, g_mem, w_kv, zeros_f, l, n_sig_cols=0,
                           tm=min(1024, batch * n_mem), tn=W_M)

        oa = _attn_a(proj, tables, l, batch=batch, seq=seq, ng=ng)
        ob = _attn_b(proj, qaug, kp, batch=batch, seq=seq, ng=ng, tq=tq_b)
        om = _attn_m(proj, kv, batch=batch, seq=seq, n_mem=n_mem, ng=ng, tq=512)

        h, xp, lt = _merge(h, proj, oa, ob, om, wa, wb, wm, wo, g_ffn, wr, br, l, tm=256)

        idx, wt, block_e, n_used = _route(lt, tn=512, nbp=nbp)
        dest = idx[2:4].reshape(2 * t)
        xb = _dispatch(dest, xp, xb, tc=tc, rt=rt)
        yb = _ffn(block_e[0, :nb], n_used[0, :1], xb, w1b, w3b, w2b, l)
        h = _combine(dest, h, wt, g_final, yb, tc=tc, rt=rt, final=(l == depth - 1))

    return h.reshape(batch, seq, d)
```
